```python
import math
import jax
import jax.numpy as jnp
from jax import lax
import numpy as np

D_MODEL = 1024
BATCH = 4
SEQ = 4096
DEPTH = 2
DEC_BATCH = 32
DEC_SEQ = 8
PAST_LEN = 16384
PAGE_SIZE = 128

HEAD_DIM = 64
D_MIX = D_MODEL
H_A = D_MIX // (4 * HEAD_DIM)
H_C = D_MIX // (4 * HEAD_DIM)
H_B = D_MIX // HEAD_DIM - H_A - H_C
W_A = H_A * HEAD_DIM
W_B = H_B * HEAD_DIM
W_C = H_C * HEAD_DIM
CHUNK = 128
Q_BLOCK = 128
LORA_W = 64
LORA_A = 64
DECAY_MAX = math.exp(-0.5)
GN_EPS = 64e-5
NORM_EPS = 1e-6
LN_EPS = 1e-5

A_U = 0
A_V = A_U + W_A
B_Q = A_V + W_A
B_K = B_Q + W_B
B_V = B_K + W_B
B_F = B_V + W_B
C_R = B_F + H_B
C_K = C_R + W_C
C_V = C_K + W_C
C_WL = C_V + W_C
C_AL = C_WL + LORA_W
Z_OFF = C_AL + LORA_A
D_IN = Z_OFF + D_MIX
SHIFT_W = Z_OFF - C_R

kernel_name = "hymba_gmlp_fox_rwkv7_adaln_step"


def _rms(x):
    xf = x.astype(jnp.float32)
    return xf * lax.rsqrt(jnp.mean(xf * xf, axis=-1, keepdims=True) + NORM_EPS)


def _layernorm(x, g, b, eps):
    xf = x.astype(jnp.float32)
    mu = jnp.mean(xf, axis=-1, keepdims=True)
    var = jnp.mean(jnp.square(xf - mu), axis=-1, keepdims=True)
    return (xf - mu) * lax.rsqrt(var + eps) * g + b


def _heads(x, n_heads):
    return x.reshape(x.shape[:-1] + (n_heads, HEAD_DIM))


def _chunk_mix(v, w_s, b_s):
    nb, t_len = v.shape[0], v.shape[1]
    n_chunks = -(-t_len // CHUNK)
    pad = n_chunks * CHUNK - t_len
    vp = jnp.pad(v, ((0, 0), (0, pad), (0, 0), (0, 0))).reshape(nb, n_chunks, CHUNK, H_A, HEAD_DIM)
    w_causal = jnp.where(jnp.tril(jnp.ones((CHUNK, CHUNK), dtype=bool)), w_s, 0).astype(v.dtype)
    mix = jnp.einsum('hts,bnshd->bnthd', w_causal, vp) + jnp.swapaxes(b_s, 0, 1)[None, None, :, :, None]
    return mix.reshape(nb, n_chunks * CHUNK, H_A, HEAD_DIM)[:, :t_len]


def _fox_block(q, cq, qpos, k, v, ck, kpos):
    s = jnp.einsum('bqhd,bkhd->bhqk', q, k).astype(jnp.float32) * (HEAD_DIM ** -0.5)
    s = s + (jnp.transpose(cq, (0, 2, 1))[..., :, None] - jnp.transpose(ck, (0, 2, 1))[..., None, :])
    s = jnp.where(kpos[None, None, None, :] <= qpos[None, None, :, None], s, -jnp.inf)
    p = jax.nn.softmax(s, axis=-1)
    return jnp.einsum('bhqk,bkhd->bqhd', p.astype(v.dtype), v)


def _attend_prompt(q, k, v, logf):
    nb, t_len = q.shape[0], q.shape[1]
    c = jnp.cumsum(logf, axis=1)
    n_blk = t_len // Q_BLOCK
    pos = jnp.arange(t_len, dtype=jnp.int32)
    qb = q.reshape(nb, n_blk, Q_BLOCK, H_B, HEAD_DIM).transpose(1, 0, 2, 3, 4)
    cqb = c.reshape(nb, n_blk, Q_BLOCK, H_B).transpose(1, 0, 2, 3)
    pb = pos.reshape(n_blk, Q_BLOCK)
    out = lax.map(lambda a: _fox_block(a[0], a[1], a[2], k, v, c, pos), (qb, cqb, pb))
    return out.transpose(1, 0, 2, 3, 4).reshape(nb, t_len, H_B, HEAD_DIM)


def _make_attend_sample(k_pool, v_pool, f_pool, page_table):
    def attend(q, k, v, logf):
        nb, t_len = q.shape[0], q.shape[1]
        past = page_table.shape[1] * PAGE_SIZE
        k_past = k_pool[page_table].reshape(nb, past, H_B, HEAD_DIM).astype(q.dtype)
        v_past = v_pool[page_table].reshape(nb, past, H_B, HEAD_DIM).astype(q.dtype)
        f_past = f_pool[page_table].reshape(nb, past, H_B).astype(jnp.float32)
        k_all = jnp.concatenate([k_past, k], axis=1)
        v_all = jnp.concatenate([v_past, v], axis=1)
        c_all = jnp.cumsum(jnp.concatenate([f_past, logf], axis=1), axis=1)
        kpos = jnp.arange(past + t_len, dtype=jnp.int32)
        qpos = past + jnp.arange(t_len, dtype=jnp.int32)
        return _fox_block(q, c_all[:, past:], qpos, k_all, v_all, c_all, kpos)
    return attend


def _rwkv7_recurrence(r, w, k, v, kk, a, s0):
    def step(S, inp):
        r_t, w_t, k_t, v_t, kk_t, a_t = inp
        sa = jnp.einsum('bhvk,bhk->bhv', S, -kk_t)
        S = S * w_t[:, :, None, :] + sa[..., None] * (kk_t * a_t)[:, :, None, :] + v_t[..., None] * k_t[:, :, None, :]
        return S, jnp.einsum('bhvk,bhk->bhv', S, r_t)
    xs = tuple(jnp.moveaxis(t, 1, 0) for t in (r, w, k, v, kk, a))
    s_fin, o = lax.scan(step, s0, xs)
    return jnp.moveaxis(o, 0, 1), s_fin


def _layer(x, c, shift0, wkv0, attend, lp):
    dt = x.dtype
    nb, t_len = x.shape[0], x.shape[1]
    mod = jax.nn.silu(c) @ lp["w_ada"] + lp["b_ada"]
    ada_shift, ada_scale, ada_gate = jnp.split(mod, 3, axis=-1)
    h = (_rms(x) * lp["g_norm"] * (1.0 + ada_scale[:, None]) + ada_shift[:, None]).astype(dt)
    p = h @ lp["w_in"]

    u_a = jax.nn.gelu(p[..., A_U:A_V])
    v_a = _layernorm(jax.nn.gelu(p[..., A_V:B_Q]), lp["a_ln_g"], lp["a_ln_b"], LN_EPS).astype(dt)
    v_a = _heads(v_a, H_A)
    y_a = u_a * _chunk_mix(v_a, lp["a_ws"], lp["a_bs"]).reshape(nb, t_len, W_A)

    q_b = (_rms(_heads(p[..., B_Q:B_K], H_B)) * lp["b_qg"]).astype(dt)
    k_b = (_rms(_heads(p[..., B_K:B_V], H_B)) * lp["b_kg"]).astype(dt)
    v_b = _heads(p[..., B_V:B_F], H_B)
    logf = jax.nn.log_sigmoid(p[..., B_F:C_R].astype(jnp.float32) + lp["b_fb"])
    y_b = attend(q_b, k_b, v_b, logf).reshape(nb, t_len, W_B)

    p_c = p[..., C_R:Z_OFF]
    prev = jnp.concatenate([shift0[:, None].astype(dt), p_c[:, :-1]], axis=1)
    m_c = (p_c + (prev - p_c) * lp["c_mu"]).astype(jnp.float32)
    r_c = m_c[..., :W_C]
    k_c = m_c[..., W_C:2 * W_C]
    v_c = m_c[..., 2 * W_C:3 * W_C]
    w_lo = m_c[..., 3 * W_C:3 * W_C + LORA_W]
    a_lo = m_c[..., 3 * W_C + LORA_W:]
    decay = jnp.exp(-DECAY_MAX * jax.nn.sigmoid(lp["c_w0"] + jnp.tanh(w_lo) @ lp["c_wb"]))
    a_c = jax.nn.sigmoid(lp["c_a0"] + a_lo @ lp["c_ab"])
    kk = _heads(k_c * lp["c_kk"], H_C)
    kk = kk * lax.rsqrt(jnp.maximum(jnp.sum(kk * kk, axis=-1, keepdims=True), 1e-24))
    k_c = k_c * (1.0 + (a_c - 1.0) * lp["c_ka"])
    r_h, k_h, v_h = _heads(r_c, H_C), _heads(k_c, H_C), _heads(v_c, H_C)
    o_c, wkv_fin = _rwkv7_recurrence(r_h, _heads(decay, H_C), k_h, v_h, kk, _heads(a_c, H_C),
                                     wkv0.astype(jnp.float32))
    o_c = _layernorm(o_c, lp["c_gn_g"], lp["c_gn_b"], GN_EPS) + jnp.sum(r_h * k_h * lp["c_rk"], axis=-1, keepdims=True) * v_h
    y_c = o_c.reshape(nb, t_len, W_C).astype(dt)

    y = jnp.concatenate([y_a, y_b, y_c], axis=-1) * jax.nn.silu(p[..., Z_OFF:])
    x_out = x + ada_gate[:, None] * (y @ lp["w_out"])
    return x_out, (k_b, v_b, logf, v_a, wkv_fin, p_c[:, -1])


def setup_inputs(seed: int = 0) -> dict:
    key = jax.random.key(seed)
    keys = jax.random.split(key, 32)

    def nrm(i, shape, s=1.0):
        return jax.random.normal(keys[i], shape, jnp.float32) * s

    n_pages = PAST_LEN // PAGE_SIZE
    n_pool = (5 * DEC_BATCH * n_pages) // 4
    perm = jax.random.permutation(keys[0], n_pool)
    page_table = perm[: DEC_BATCH * n_pages].reshape(DEC_BATCH, n_pages).astype(jnp.int32)
    L = DEPTH
    return {
        "x_prompt": nrm(1, (BATCH, SEQ, D_MODEL)),
        "x_sample": nrm(2, (DEC_BATCH, DEC_SEQ, D_MODEL)),
        "c_prompt": nrm(3, (BATCH, D_MODEL)),
        "c_sample": nrm(4, (DEC_BATCH, D_MODEL)),
        "cache_fox_k": nrm(5, (L, n_pool, PAGE_SIZE, H_B, HEAD_DIM)),
        "cache_fox_v": nrm(6, (L, n_pool, PAGE_SIZE, H_B, HEAD_DIM)),
        "cache_fox_logf": jax.nn.log_sigmoid(3.0 + nrm(7, (L, n_pool, PAGE_SIZE, H_B))),
        "page_table": page_table,
        "state_wkv": nrm(8, (L, DEC_BATCH, H_C, HEAD_DIM, HEAD_DIM), 0.5),
        "state_shift": nrm(9, (L, DEC_BATCH, SHIFT_W)),
        "w_ada": nrm(10, (L, D_MODEL, 3 * D_MODEL), 0.5 * D_MODEL ** -0.5),
        "b_ada": nrm(11, (L, 3 * D_MODEL), 0.01),
        "g_norm": 1.0 + nrm(12, (L, D_MODEL), 0.1),
        "w_in": nrm(13, (L, D_MODEL, D_IN), D_MODEL ** -0.5),
        "w_out": nrm(14, (L, D_MIX, D_MODEL), D_MIX ** -0.5),
        "a_ln_g": 1.0 + nrm(15, (L, W_A), 0.1),
        "a_ln_b": nrm(16, (L, W_A), 0.01),
        "a_ws": nrm(17, (L, H_A, CHUNK, CHUNK), CHUNK ** -0.5),
        "a_bs": 1.0 + nrm(18, (L, H_A, CHUNK), 0.1),
        "b_qg": 1.0 + nrm(19, (L, HEAD_DIM), 0.1),
        "b_kg": 1.0 + nrm(20, (L, HEAD_DIM), 0.1),
        "b_fb": 3.0 + nrm(21, (L, H_B), 0.5),
        "c_mu": jax.random.uniform(keys[22], (L, SHIFT_W), jnp.float32),
        "c_w0": nrm(23, (L, W_C), 0.5),
        "c_wb": nrm(24, (L, LORA_W, W_C), 0.5 * LORA_W ** -0.5),
        "c_a0": nrm(25, (L, W_C), 0.5),
        "c_ab": nrm(26, (L, LORA_A, W_C), 0.5 * LORA_A ** -0.5),
        "c_kk": 0.85 + nrm(27, (L, W_C), 0.05),
        "c_ka": 1.0 + nrm(28, (L, W_C), 0.05),
        "c_rk": nrm(29, (L, H_C, HEAD_DIM), 0.1),
        "c_gn_g": 1.0 + nrm(30, (L, H_C, HEAD_DIM), 0.1),
        "c_gn_b": nrm(31, (L, H_C, HEAD_DIM), 0.01),
    }


def reference(x_prompt, x_sample, c_prompt, c_sample, cache_fox_k, cache_fox_v, cache_fox_logf,
              page_table, state_wkv, state_shift, w_ada, b_ada, g_norm, w_in, w_out, a_ln_g, a_ln_b,
              a_ws, a_bs, b_qg, b_kg, b_fb, c_mu, c_w0, c_wb, c_a0, c_ab, c_kk, c_ka, c_rk,
              c_gn_g, c_gn_b):
    xp, xs = x_prompt, x_sample
    kp, vp, fp, wp, sp = [], [], [], [], []
    ks, vs, fs, cvs, wsl, ssl = [], [], [], [], [], []
    for l in range(DEPTH):
        lp = dict(w_ada=w_ada[l], b_ada=b_ada[l], g_norm=g_norm[l], w_in=w_in[l], w_out=w_out[l],
                  a_ln_g=a_ln_g[l], a_ln_b=a_ln_b[l], a_ws=a_ws[l], a_bs=a_bs[l],
                  b_qg=b_qg[l], b_kg=b_kg[l], b_fb=b_fb[l],
                  c_mu=c_mu[l], c_w0=c_w0[l], c_wb=c_wb[l], c_a0=c_a0[l], c_ab=c_ab[l],
                  c_kk=c_kk[l], c_ka=c_ka[l], c_rk=c_rk[l], c_gn_g=c_gn_g[l], c_gn_b=c_gn_b[l])
        shift0_p = jnp.zeros((xp.shape[0], SHIFT_W), xp.dtype)
        wkv0_p = jnp.zeros((xp.shape[0], H_C, HEAD_DIM, HEAD_DIM), jnp.float32)
        xp, st_p = _layer(xp, c_prompt, shift0_p, wkv0_p, _attend_prompt, lp)
        kp.append(st_p[0]); vp.append(st_p[1]); fp.append(st_p[2]); wp.append(st_p[4]); sp.append(st_p[5])
        attend_s = _make_attend_sample(cache_fox_k[l], cache_fox_v[l], cache_fox_logf[l], page_table)
        xs, st_s = _layer(xs, c_sample, state_shift[l], state_wkv[l], attend_s, lp)
        ks.append(st_s[0]); vs.append(st_s[1]); fs.append(st_s[2]); cvs.append(st_s[3])
        wsl.append(st_s[4]); ssl.append(st_s[5])
    return (xp, xs,
            jnp.stack(kp), jnp.stack(vp), jnp.stack(fp),
            jnp.stack(ks), jnp.stack(vs), jnp.stack(fs),
            jnp.stack(cvs),
            jnp.stack(wp), jnp.stack(wsl),
            jnp.stack(sp), jnp.stack(ssl))
```

```python
import functools
import math

import jax
import jax.numpy as jnp
from jax import lax
from jax.experimental import pallas as pl
from jax.experimental.pallas import tpu as pltpu

F32 = jnp.float32
BF16 = jnp.bfloat16
HEAD_DIM = 64
LANES = 128
NORM_EPS = 1e-6
LN_EPS = 1e-5
GN_EPS = 64e-5
DECAY_MAX = math.exp(-0.5)
NEG_BIG = -1e30
VMEM_LIMIT_BYTES = 56 * 1024 * 1024
HIGHEST = lax.Precision.HIGHEST


def _sigmoid(x):
    return 1.0 / (1.0 + jnp.exp(-x))


def _silu(x):
    return x * _sigmoid(x)


def _gelu_tanh(x):
    return 0.5 * x * (1.0 + jnp.tanh(math.sqrt(2.0 / math.pi) * (x + 0.044715 * (x * x * x))))


def _log_sigmoid(x):
    return jnp.minimum(x, 0.0) - jnp.log1p(jnp.exp(-jnp.abs(x)))


def _mm(a, b):
    return jnp.dot(a.astype(BF16), b.astype(BF16), preferred_element_type=F32)


def _mm_nt(a, b):
    return lax.dot_general(a.astype(BF16), b.astype(BF16), (((1,), (1,)), ((), ())),
                           preferred_element_type=F32)


def _mm_tn(a, b):
    return lax.dot_general(a.astype(BF16), b.astype(BF16), (((0,), (0,)), ((), ())),
                           preferred_element_type=F32)


def _split3(x):
    hi = x.astype(BF16)
    r1 = x - hi.astype(F32)
    mid = r1.astype(BF16)
    lo = (r1 - mid.astype(F32)).astype(BF16)
    return hi, mid, lo


def _mm_exact_l(m01, x):
    m = m01.astype(BF16)
    hi, mid, lo = _split3(x)
    d = lambda p: jnp.dot(m, p, preferred_element_type=F32)
    return d(hi) + d(mid) + d(lo)


def _mm_exact_r(x, m01):
    m = m01.astype(BF16)
    hi, mid, lo = _split3(x)
    d = lambda p: jnp.dot(p, m, preferred_element_type=F32)
    return d(hi) + d(mid) + d(lo)


def _iota(shape, dim):
    return lax.broadcasted_iota(jnp.int32, shape, dim)


def _cparams(sem):
    return pltpu.CompilerParams(dimension_semantics=sem, vmem_limit_bytes=VMEM_LIMIT_BYTES)


def _ada_kernel(c_ref, w_ref, b_ref, o_ref):
    c = c_ref[...]
    o_ref[...] = jnp.dot(_silu(c), w_ref[...], precision=HIGHEST, preferred_element_type=F32) + b_ref[...]


def _ada_mod(c_all, w_ada, b_ada):
    n, d = c_all.shape
    nl = w_ada.shape[0]
    return pl.pallas_call(
        _ada_kernel,
        grid=(nl, 3),
        in_specs=[pl.BlockSpec((n, d), lambda l, j: (0, 0)),
                  pl.BlockSpec((None, d, d), lambda l, j: (l, 0, j)),
                  pl.BlockSpec((None, 1, d), lambda l, j: (l, 0, j))],
        out_specs=pl.BlockSpec((None, n, d), lambda l, j: (l, 0, j)),
        out_shape=jax.ShapeDtypeStruct((nl, n, 3 * d), F32),
        compiler_params=_cparams(("arbitrary", "arbitrary")),
        name="ada_mod",
    )(c_all, w_ada, b_ada.reshape(nl, 1, 3 * d))


def _head_rms_rows(x, ones_bd, gain):
    xx = x * x
    hi = xx.astype(BF16)
    lo = (xx - hi.astype(F32)).astype(BF16)
    ss = jnp.dot(hi, ones_bd, preferred_element_type=F32) + jnp.dot(lo, ones_bd, preferred_element_type=F32)
    return x * lax.rsqrt(ss * (1.0 / HEAD_DIM) + NORM_EPS) * gain


def _in_kernel(x_ref, mod_ref, g_ref, wa_ref, wq_ref, wk_ref, wv_ref, wc_ref, wz_ref, wf_ref, ones_ref,
               qg_ref, kg_ref, fb_ref, pa_ref, q_ref, k_ref, v_ref, f_ref, pc_ref, z_ref, *, kv_t):
    x = x_ref[...]
    nb, tt, d = x.shape
    m = nb * tt
    xn = x * lax.rsqrt(jnp.mean(x * x, axis=-1, keepdims=True) + NORM_EPS)
    mod = mod_ref[...]
    h = xn * g_ref[...] * (1.0 + mod[:, :, d:2 * d]) + mod[:, :, :d]
    hb = h.reshape(m, d).astype(BF16)

    def proj(w_ref):
        return jnp.dot(hb, w_ref[...], preferred_element_type=F32)

    def put(ref, val):
        ref[...] = val.reshape(nb, tt, val.shape[-1])

    put(pa_ref, proj(wa_ref))
    put(pc_ref, proj(wc_ref))
    put(z_ref, _silu(proj(wz_ref)))
    put(f_ref, _log_sigmoid(proj(wf_ref) + fb_ref[...]))
    put(q_ref, _head_rms_rows(proj(wq_ref), ones_ref[...], qg_ref[...]) * (HEAD_DIM ** -0.5))
    if kv_t:
        kt = lax.dot_general(wk_ref[...], hb, (((1,), (1,)), ((), ())), preferred_element_type=F32)
        nh = kt.shape[0] // HEAD_DIM
        kt3 = kt.reshape(nh, HEAD_DIM, m)
        ms = jnp.mean(kt3 * kt3, axis=1, keepdims=True)
        kt3 = kt3 * lax.rsqrt(ms + NORM_EPS) * kg_ref[...][None]
        k_ref[0] = kt3.reshape(nh * HEAD_DIM, m)
        v_ref[0] = lax.dot_general(wv_ref[...], hb, (((1,), (1,)), ((), ())), preferred_element_type=F32)
    else:
        put(k_ref, _head_rms_rows(proj(wk_ref), ones_ref[...], kg_ref[...]))
        put(v_ref, proj(wv_ref))


def _in_proj(x, mod, w, *, kv_t, nb_blk, t_blk):
    b, t, d = x.shape
    wb = w["wq"].shape[1]
    grid = (b // nb_blk, t // t_blk)
    row = lambda width: pl.BlockSpec((nb_blk, t_blk, width), lambda i, j: (i, j, 0))
    full = lambda a: pl.BlockSpec(a.shape, lambda i, j: (0,) * a.ndim)
    wa, wq, wk, wv, wc, wz, wf = (w[n] for n in ("wa", "wq", "wk", "wv", "wc", "wz", "wf"))
    if kv_t:
        assert nb_blk == 1
        kv_spec = pl.BlockSpec((1, wb, t_blk), lambda i, j: (i, 0, j))
        kv_shape = jax.ShapeDtypeStruct((b, wb, t), F32)
        kg = w["kg_col"]
    else:
        kv_spec = row(wb)
        kv_shape = jax.ShapeDtypeStruct((b, t, wb), F32)
        kg = w["kg_row"]
    outs = [
        (wa.shape[1], row(wa.shape[1])), (wb, row(wb)), None, None,
        (LANES, row(LANES)), (wc.shape[1], row(wc.shape[1])), (wz.shape[1], row(wz.shape[1]))]
    out_shape, out_specs = [], []
    for o in outs:
        if o is None:
            out_shape.append(kv_shape)
            out_specs.append(kv_spec)
        else:
            out_shape.append(jax.ShapeDtypeStruct((b, t, o[0]), F32))
            out_specs.append(o[1])
    return pl.pallas_call(
        functools.partial(_in_kernel, kv_t=kv_t),
        grid=grid,
        in_specs=[row(d), pl.BlockSpec((nb_blk, 1, 3 * d), lambda i, j: (i, 0, 0)), full(w["g"]),
                  full(wa), full(wq), full(wk), full(wv), full(wc), full(wz), full(wf), full(w["ones_bd"]),
                  full(w["qg_row"]), full(kg), full(w["fb"])],
        out_specs=out_specs,
        out_shape=out_shape,
        compiler_params=_cparams(("arbitrary", "arbitrary")),
        name="in_proj_t" if kv_t else "in_proj",
    )(x, mod, w["g"], wa, wq, wk, wv, wc, wz, wf, w["ones_bd"], w["qg_row"], kg, w["fb"])


def _mix_kernel(pa_ref, lng_ref, lnb_ref, ws_ref, bs_ref, ya_ref, va_ref, *, lowp):
    pa = pa_ref[0]
    tc = pa.shape[0]
    wa = pa.shape[1] // 2
    nh = wa // HEAD_DIM
    u = _gelu_tanh(pa[:, :wa])
    g = _gelu_tanh(pa[:, wa:])
    mu = jnp.mean(g, axis=-1, keepdims=True)
    var = jnp.mean(jnp.square(g - mu), axis=-1, keepdims=True)
    va = (g - mu) * lax.rsqrt(var + LN_EPS) * lng_ref[...] + lnb_ref[...]
    va_ref[0] = va
    causal = _iota((tc, tc), 1) <= _iota((tc, tc), 0)
    lane_head = _iota((1, wa), 1) // HEAD_DIM
    mix = jnp.zeros((tc, wa), F32)
    for h in range(nh):
        wc = jnp.where(causal, ws_ref[h, :tc, :tc], 0.0)
        vh = jnp.where(lane_head == h, va, 0.0)
        if lowp:
            mix = mix + _mm(wc, vh)
        else:
            mix = mix + jnp.dot(wc, vh, preferred_element_type=F32)
    ya_ref[0] = u * (mix + bs_ref[:tc, :])


def _chunk_mix(pa, w):
    b, t, w2 = pa.shape
    chunk = w["a_ws"].shape[-1]
    tc = min(chunk, t)
    assert t % tc == 0
    wa = w2 // 2
    full = lambda a: pl.BlockSpec(a.shape, lambda i, j: (0,) * a.ndim)
    return pl.pallas_call(
        functools.partial(_mix_kernel, lowp=tc >= 16),
        grid=(b, t // tc),
        in_specs=[pl.BlockSpec((1, tc, w2), lambda i, j: (i, j, 0)), full(w["a_ln_g"]), full(w["a_ln_b"]),
                  full(w["a_ws"]), full(w["a_bs_exp"])],
        out_specs=[pl.BlockSpec((1, tc, wa), lambda i, j: (i, j, 0))] * 2,
        out_shape=[jax.ShapeDtypeStruct((b, t, wa), F32)] * 2,
        compiler_params=_cparams(("arbitrary", "arbitrary")),
        name="chunk_mix",
    )(pa, w["a_ln_g"], w["a_ln_b"], w["a_ws"], w["a_bs_exp"])


def _cum_kernel(f_ref, c_ref, ct_ref, ft_ref, carry, *, nh):
    @pl.when(pl.program_id(1) == 0)
    def _():
        carry[...] = jnp.zeros_like(carry)

    f = f_ref[0]
    tb = f.shape[0]
    tri = (_iota((tb, tb), 1) <= _iota((tb, tb), 0)).astype(F32)
    c = _mm_exact_l(tri, f) + carry[...]
    carry[...] = c[tb - 1:tb, :]
    c_ref[0] = c
    ct_ref[0] = c.T[:nh, :]
    ft_ref[0] = f.T[:nh, :]


def _cumsum_logf(f, nh, tb):
    b, t, _ = f.shape
    row = pl.BlockSpec((1, tb, LANES), lambda i, j: (i, j, 0))
    tr = pl.BlockSpec((1, nh, tb), lambda i, j: (i, 0, j))
    return pl.pallas_call(
        functools.partial(_cum_kernel, nh=nh),
        grid=(b, t // tb),
        in_specs=[row],
        out_specs=[row, tr, tr],
        out_shape=[jax.ShapeDtypeStruct((b, t, LANES), F32), jax.ShapeDtypeStruct((b, nh, t), F32),
                   jax.ShapeDtypeStruct((b, nh, t), F32)],
        scratch_shapes=[pltpu.VMEM((1, LANES), F32)],
        compiler_params=_cparams(("arbitrary", "arbitrary")),
        name="cumsum_logf",
    )(f)


def _foxp_kernel(q_ref, kt_ref, vt_ref, c_ref, ct_ref, o_ref, m_scr, l_scr, acc_scr, *, nh, tq):
    qi = pl.program_id(1)
    ki = pl.program_id(2)

    @pl.when(ki == 0)
    def _():
        m_scr[...] = jnp.full_like(m_scr, NEG_BIG)
        l_scr[...] = jnp.zeros_like(l_scr)
        acc_scr[...] = jnp.zeros_like(acc_scr)

    lane = _iota((1, LANES), 1)
    first_head = lane < HEAD_DIM

    def step(masked):
        q = q_ref[0]
        cq = c_ref[0]
        ck = ct_ref[0]
        if masked:
            keep = _iota((tq, tq), 1) <= _iota((tq, tq), 0)
        for pr in range(nh // 2):
            sl = slice(pr * LANES, (pr + 1) * LANES)
            qp = q[:, sl]
            ktp = kt_ref[0, sl, :].astype(BF16)
            vtp = vt_ref[0, sl, :].astype(BF16)
            alphas, pvs = [], []
            for hh in range(2):
                h = 2 * pr + hh
                qh = jnp.where(first_head == (hh == 0), qp, 0.0).astype(BF16)
                s = jnp.dot(qh, ktp, preferred_element_type=F32) + cq[:, h:h + 1] - ck[h:h + 1, :]
                if masked:
                    s = jnp.where(keep, s, NEG_BIG)
                m_prev = m_scr[h]
                m_new = jnp.maximum(m_prev, jnp.max(s, axis=-1, keepdims=True))
                alpha = jnp.exp(m_prev - m_new)
                p = jnp.exp(s - m_new)
                l_scr[h] = alpha * l_scr[h] + jnp.sum(p, axis=-1, keepdims=True)
                m_scr[h] = m_new
                alphas.append(alpha)
                pvs.append(lax.dot_general(p.astype(BF16), vtp, (((1,), (1,)), ((), ())),
                                           preferred_element_type=F32))
            a = jnp.where(first_head, alphas[0], alphas[1])
            pv = jnp.where(first_head, pvs[0], pvs[1])
            acc_scr[:, sl] = acc_scr[:, sl] * a + pv

    @pl.when(ki < qi)
    def _():
        step(False)

    @pl.when(ki == qi)
    def _():
        step(True)
        for pr in range(nh // 2):
            sl = slice(pr * LANES, (pr + 1) * LANES)
            l = jnp.where(first_head, l_scr[2 * pr], l_scr[2 * pr + 1])
            o_ref[0, :, sl] = acc_scr[:, sl] / l


def _fox_prompt(q, kt, vt, c, ct, nh, tq):
    b, t, wb = q.shape
    nq = t // tq
    kv_idx = lambda i, j, k: (i, 0, jnp.minimum(k, j))
    return pl.pallas_call(
        functools.partial(_foxp_kernel, nh=nh, tq=tq),
        grid=(b, nq, nq),
        in_specs=[pl.BlockSpec((1, tq, wb), lambda i, j, k: (i, j, 0)),
                  pl.BlockSpec((1, wb, tq), kv_idx),
                  pl.BlockSpec((1, wb, tq), kv_idx),
                  pl.BlockSpec((1, tq, LANES), lambda i, j, k: (i, j, 0)),
                  pl.BlockSpec((1, nh, tq), kv_idx)],
        out_specs=pl.BlockSpec((1, tq, wb), lambda i, j, k: (i, j, 0)),
        out_shape=jax.ShapeDtypeStruct((b, t, wb), F32),
        scratch_shapes=[pltpu.VMEM((nh, tq, 1), F32), pltpu.VMEM((nh, tq, 1), F32), pltpu.VMEM((tq, wb), F32)],
        compiler_params=_cparams(("arbitrary", "arbitrary", "arbitrary")),
        name="fox_prompt",
    )(q, kt, vt, c, ct)


def _cums_kernel(pt_ref, f_hbm, o_ref, buf, sem, *, layer, npg, nh):
    b = pl.program_id(0)

    def page_copy(j, page):
        return pltpu.make_async_copy(f_hbm.at[layer, page], buf.at[j], sem)

    def start(j, carry):
        page_copy(j, pt_ref[b, j]).start()
        return carry

    def wait(j, carry):
        page_copy(j, 0).wait()
        return carry

    lax.fori_loop(0, npg, start, 0)
    lax.fori_loop(0, npg, wait, 0)
    ps = buf.shape[-1]
    tri_keys = (_iota((ps, ps), 0) <= _iota((ps, ps), 1)).astype(F32)
    tri_pages = (_iota((npg, npg), 1) < _iota((npg, npg), 0)).astype(F32)
    for h in range(nh):
        xh = buf[:, h, :]
        y = _mm_exact_r(xh, tri_keys)
        tot = jnp.broadcast_to(y[:, ps - 1:ps], (npg, ps))
        o_ref[0, :, h, :] = y + _mm_exact_l(tri_pages, tot)


def _cumsum_pages(page_table, logf_t, layer):
    bs, npg = page_table.shape
    _, _, nh, ps = logf_t.shape
    return pl.pallas_call(
        functools.partial(_cums_kernel, layer=layer, npg=npg, nh=nh),
        grid_spec=pltpu.PrefetchScalarGridSpec(
            num_scalar_prefetch=1,
            grid=(bs,),
            in_specs=[pl.BlockSpec(memory_space=pl.ANY)],
            out_specs=pl.BlockSpec((1, npg, nh, ps), lambda i, pt: (i, 0, 0, 0)),
            scratch_shapes=[pltpu.VMEM((npg, nh, ps), F32), pltpu.SemaphoreType.DMA(())],
        ),
        out_shape=jax.ShapeDtypeStruct((bs, npg, nh, ps), F32),
        compiler_params=_cparams(("arbitrary",)),
        name="cumsum_pages",
    )(page_table, logf_t)


def _foxs_kernel(pt_ref, q_ref, kn_ref, vn_ref, fn_ref, cs_ref, cl_ref, *rest, nh, ts, pp):
    kt_refs = rest[:pp]
    vt_refs = rest[pp:2 * pp]
    o_ref = rest[2 * pp]
    m_scr, l_scr, acc_scr, cq_scr = rest[2 * pp + 1:]
    s_idx = pl.program_id(1)
    n_steps = pl.num_programs(1)
    nr = nh * ts
    wb = nh * HEAD_DIM
    ps = cs_ref.shape[-1]
    row_head = _iota((nr, 1), 0) // ts
    row_q = _iota((nr, 1), 0) % ts

    def rows_from_heads(x):
        return jnp.concatenate([jnp.broadcast_to(x[h:h + 1, :], (ts, x.shape[1])) for h in range(nh)], axis=0)

    q = q_ref[0]
    q_rep = jnp.concatenate([q] * nh, axis=0)
    qbd = jnp.where(_iota((nr, wb), 1) // HEAD_DIM == row_head, q_rep, 0.0).astype(BF16)

    fn = fn_ref[0]
    f_pad = jnp.concatenate([fn, jnp.zeros((LANES - ts, LANES), F32)], axis=0)
    a_new = rows_from_heads(f_pad.T[:nh, :])
    tri = (_iota((LANES, LANES), 0) <= _iota((LANES, LANES), 1)).astype(F32)
    c_new = _mm_exact_r(a_new, tri)

    @pl.when(s_idx == 0)
    def _():
        m_scr[...] = jnp.full_like(m_scr, NEG_BIG)
        l_scr[...] = jnp.zeros_like(l_scr)
        acc_scr[...] = jnp.zeros_like(acc_scr)
        total = rows_from_heads(cl_ref[0, 0][:, ps - 1:ps])
        own = jnp.sum(jnp.where(_iota((nr, LANES), 1) == row_q, c_new, 0.0), axis=-1, keepdims=True)
        cq_scr[...] = total + own

    cq = cq_scr[...]

    def online(s_list, v_list, v_is_t):
        m_prev = m_scr[...]
        m_new = m_prev
        for s in s_list:
            m_new = jnp.maximum(m_new, jnp.max(s, axis=-1, keepdims=True))
        alpha = jnp.exp(m_prev - m_new)
        l_add = jnp.zeros_like(m_prev)
        pv = jnp.zeros((nr, wb), F32)
        for s, v in zip(s_list, v_list):
            p = jnp.exp(s - m_new)
            l_add = l_add + jnp.sum(p, axis=-1, keepdims=True)
            if v_is_t:
                pv = pv + lax.dot_general(p.astype(BF16), v, (((1,), (1,)), ((), ())),
                                          preferred_element_type=F32)
            else:
                pv = pv + jnp.dot(p.astype(BF16), v, preferred_element_type=F32)
        m_scr[...] = m_new
        l_scr[...] = alpha * l_scr[...] + l_add
        acc_scr[...] = alpha * acc_scr[...] + pv

    s_list, v_list = [], []
    for j in range(pp):
        ck = rows_from_heads(cs_ref[0, j])
        s = jnp.dot(qbd, kt_refs[j][...].astype(BF16), preferred_element_type=F32) + cq - ck
        s_list.append(s)
        v_list.append(vt_refs[j][...].astype(BF16))
    online(s_list, v_list, True)

    @pl.when(s_idx == n_steps - 1)
    def _():
        kn = jnp.concatenate([kn_ref[0], jnp.zeros((LANES - ts, wb), F32)], axis=0).astype(BF16)
        vn = jnp.concatenate([vn_ref[0], jnp.zeros((LANES - ts, wb), F32)], axis=0).astype(BF16)
        total = rows_from_heads(cl_ref[0, 0][:, ps - 1:ps])
        s = lax.dot_general(qbd, kn, (((1,), (1,)), ((), ())), preferred_element_type=F32) + cq - (total + c_new)
        s = jnp.where(_iota((nr, LANES), 1) <= row_q, s, NEG_BIG)
        online([s], [vn], False)
        o_full = acc_scr[...] / l_scr[...]
        lane_head = _iota((1, wb), 1) // HEAD_DIM
        out = jnp.zeros((ts, wb), F32)
        for h in range(nh):
            out = out + jnp.where(lane_head == h, o_full[h * ts:(h + 1) * ts, :], 0.0)
        o_ref[0] = out


def _fox_sample(page_table, q, kn, vn, fn, cs, kt_pool, vt_pool, layer, nh):
    bs, ts, wb = q.shape
    npg = page_table.shape[1]
    ps = kt_pool.shape[-1]
    pp = min(16, npg)
    assert npg % pp == 0
    nr = nh * ts

    def page_spec(j):
        return pl.BlockSpec((None, None, wb, ps), lambda b, s, pt: (layer, pt[b, s * pp + j], 0, 0))

    row = lambda width: pl.BlockSpec((1, ts, width), lambda b, s, pt: (b, 0, 0))
    in_specs = [row(wb), row(wb), row(wb), row(LANES),
                pl.BlockSpec((1, pp, nh, ps), lambda b, s, pt: (b, s, 0, 0)),
                pl.BlockSpec((1, 1, nh, ps), lambda b, s, pt: (b, npg - 1, 0, 0))]
    in_specs += [page_spec(j) for j in range(pp)] * 2
    return pl.pallas_call(
        functools.partial(_foxs_kernel, nh=nh, ts=ts, pp=pp),
        grid_spec=pltpu.PrefetchScalarGridSpec(
            num_scalar_prefetch=1,
            grid=(bs, npg // pp),
            in_specs=in_specs,
            out_specs=row(wb),
            scratch_shapes=[pltpu.VMEM((nr, 1), F32), pltpu.VMEM((nr, 1), F32), pltpu.VMEM((nr, wb), F32),
                            pltpu.VMEM((nr, 1), F32)],
        ),
        out_shape=jax.ShapeDtypeStruct((bs, ts, wb), F32),
        compiler_params=_cparams(("arbitrary", "arbitrary")),
        name="fox_sample",
    )(page_table, q, kn, vn, fn, cs, cs, *([kt_pool] * pp), *([vt_pool] * pp))


def _rwkv_kernel(pc_ref, sh0_ref, s0_ref, mu_ref, w0_ref, wb_ref, a0_ref, ab_ref, kkw_ref, ka_ref, rk_ref,
                 gng_ref, gnb_ref, y_ref, sfin_ref, s_scr, prev_scr, *, nh, lowp):
    ci = pl.program_id(1)

    @pl.when(ci == 0)
    def _():
        s_scr[...] = s0_ref[0]
        prev_scr[...] = sh0_ref[0]

    pc = pc_ref[0]
    c = pc.shape[0]
    wc = nh * HEAD_DIM
    lora = wb_ref.shape[0]
    rows = _iota((c, 1), 0)
    prev = jnp.where(rows == 0, prev_scr[...], pltpu.roll(pc, 1, axis=0))
    prev_scr[...] = pc[c - 1:c, :]
    m = pc + (prev - pc) * mu_ref[...]
    r_all = m[:, :wc]
    k_raw = m[:, wc:2 * wc]
    v_all = m[:, 2 * wc:3 * wc]
    w_lo = m[:, 3 * wc:3 * wc + lora]
    a_lo = m[:, 3 * wc + lora:]
    hdot = lambda a, b: jnp.dot(a, b, precision=HIGHEST, preferred_element_type=F32)
    lw_all = -DECAY_MAX * _sigmoid(w0_ref[...] + hdot(jnp.tanh(w_lo), wb_ref[...]))
    a_all = _sigmoid(a0_ref[...] + hdot(a_lo, ab_ref[...]))
    kkr_all = k_raw * kkw_ref[...]
    k_all = k_raw * (1.0 + (a_all - 1.0) * ka_ref[...])
    rk_all = r_all * k_all * rk_ref[...]

    if lowp:
        mm, mm_nt, mm_tn = _mm, _mm_nt, _mm_tn
        cumsum = lambda x: _mm_exact_l(tri_incl, x)
    else:
        mm = lambda a, b: jnp.dot(a, b, preferred_element_type=F32)
        mm_nt = lambda a, b: lax.dot_general(a, b, (((1,), (1,)), ((), ())), preferred_element_type=F32)
        mm_tn = lambda a, b: lax.dot_general(a, b, (((0,), (0,)), ((), ())), preferred_element_type=F32)
        cumsum = lambda x: jnp.dot(tri_incl, x, precision=HIGHEST, preferred_element_type=F32)
    rr = _iota((c, c), 0)
    cc = _iota((c, c), 1)
    incl = cc <= rr
    strict = cc < rr
    tri_incl = incl.astype(F32)
    eye = (cc == rr).astype(F32)

    ys = []
    for h in range(nh):
        sl = slice(h * HEAD_DIM, (h + 1) * HEAD_DIM)
        r, lw, k, v, a, kkr = r_all[:, sl], lw_all[:, sl], k_all[:, sl], v_all[:, sl], a_all[:, sl], kkr_all[:, sl]
        kk = kkr * lax.rsqrt(jnp.maximum(jnp.sum(kkr * kkr, axis=-1, keepdims=True), 1e-24))
        cum = cumsum(lw)
        e_pos = jnp.exp(cum)
        e_neg = jnp.exp(-cum)
        at = -kk * jnp.exp(cum - lw)
        bt = kk * a * e_neg
        kt = k * e_neg
        rt = r * e_pos
        l_ab = jnp.where(strict, mm_nt(at, bt), 0.0)
        l_ak = jnp.where(strict, mm_nt(at, kt), 0.0)
        m_rb = jnp.where(incl, mm_nt(rt, bt), 0.0)
        m_rk = jnp.where(incl, mm_nt(rt, kt), 0.0)
        t_inv = eye + l_ab
        l_pow = l_ab
        n = 1
        while 2 * n < c:
            l_pow = mm(l_pow, l_pow)
            t_inv = mm(t_inv, eye + l_pow)
            n *= 2
        s0 = s_scr[h]
        u = mm(t_inv, mm_nt(at, s0) + mm(l_ak, v))
        o = mm_nt(rt, s0) + mm(m_rb, u) + mm(m_rk, v)
        s_scr[h] = (s0 + mm_tn(u, bt) + mm_tn(v, kt)) * e_pos[c - 1:c, :]
        mu_o = jnp.mean(o, axis=-1, keepdims=True)
        var = jnp.mean(jnp.square(o - mu_o), axis=-1, keepdims=True)
        y = (o - mu_o) * lax.rsqrt(var + GN_EPS) * gng_ref[:, sl] + gnb_ref[:, sl]
        ys.append(y + jnp.sum(rk_all[:, sl], axis=-1, keepdims=True) * v)
    y_ref[0] = jnp.concatenate(ys, axis=-1)

    @pl.when(ci == pl.num_programs(1) - 1)
    def _():
        sfin_ref[0] = s_scr[...]


def _rwkv(pc, shift0, wkv0, w, chunk):
    b, t, ws = pc.shape
    nh = wkv0.shape[1]
    wc = nh * HEAD_DIM
    c = min(chunk, t)
    assert t % c == 0
    full = lambda a: pl.BlockSpec(a.shape, lambda i, j: (0,) * a.ndim)
    names = ("c_mu", "c_w0", "c_wb", "c_a0", "c_ab", "c_kk", "c_ka", "c_rk", "c_gn_g", "c_gn_b")
    params = [w[n] for n in names]
    return pl.pallas_call(
        functools.partial(_rwkv_kernel, nh=nh, lowp=c >= 16),
        grid=(b, t // c),
        in_specs=[pl.BlockSpec((1, c, ws), lambda i, j: (i, j, 0)),
                  pl.BlockSpec((1, 1, ws), lambda i, j: (i, 0, 0)),
                  pl.BlockSpec((1, nh, HEAD_DIM, HEAD_DIM), lambda i, j: (i, 0, 0, 0))] + [full(p) for p in params],
        out_specs=[pl.BlockSpec((1, c, wc), lambda i, j: (i, j, 0)),
                   pl.BlockSpec((1, nh, HEAD_DIM, HEAD_DIM), lambda i, j: (i, 0, 0, 0))],
        out_shape=[jax.ShapeDtypeStruct((b, t, wc), F32), jax.ShapeDtypeStruct((b, nh, HEAD_DIM, HEAD_DIM), F32)],
        scratch_shapes=[pltpu.VMEM((nh, HEAD_DIM, HEAD_DIM), F32), pltpu.VMEM((1, ws), F32)],
        compiler_params=_cparams(("arbitrary", "arbitrary")),
        name="rwkv7",
    )(pc, shift0, wkv0, *params)


def _out_kernel(ya_ref, yb_ref, yc_ref, z_ref, x_ref, mod_ref, wo_ref, o_ref):
    x = x_ref[...]
    nb, tt, d = x.shape
    m = nb * tt
    z = z_ref[...]
    wa = ya_ref.shape[-1]
    wb = yb_ref.shape[-1]
    ya = (ya_ref[...] * z[:, :, :wa]).reshape(m, wa).astype(BF16)
    yb = (yb_ref[...] * z[:, :, wa:wa + wb]).reshape(m, wb).astype(BF16)
    yc = (yc_ref[...] * z[:, :, wa + wb:]).reshape(m, -1).astype(BF16)
    y = (jnp.dot(ya, wo_ref[:wa, :], preferred_element_type=F32)
         + jnp.dot(yb, wo_ref[wa:wa + wb, :], preferred_element_type=F32)
         + jnp.dot(yc, wo_ref[wa + wb:, :], preferred_element_type=F32))
    o_ref[...] = x + mod_ref[...][:, :, 2 * d:] * y.reshape(nb, tt, d)


def _out_proj(ya, yb, yc, z, x, mod, wo, *, nb_blk, t_blk):
    b, t, d = x.shape
    row = lambda a: pl.BlockSpec((nb_blk, t_blk, a.shape[-1]), lambda i, j: (i, j, 0))
    return pl.pallas_call(
        _out_kernel,
        grid=(b // nb_blk, t // t_blk),
        in_specs=[row(ya), row(yb), row(yc), row(z), row(x),
                  pl.BlockSpec((nb_blk, 1, 3 * d), lambda i, j: (i, 0, 0)),
                  pl.BlockSpec(wo.shape, lambda i, j: (0, 0))],
        out_specs=row(x),
        out_shape=jax.ShapeDtypeStruct((b, t, d), F32),
        compiler_params=_cparams(("arbitrary", "arbitrary")),
        name="out_proj",
    )(ya, yb, yc, z, x, mod, wo)


def _layer_weights(l, w_in, w_out, g_norm, a_ln_g, a_ln_b, a_ws, a_bs, b_qg, b_kg, b_fb, c_mu, c_w0, c_wb,
                   c_a0, c_ab, c_kk, c_ka, c_rk, c_gn_g, c_gn_b):
    wa_w = a_ln_g.shape[-1]
    nhb = b_fb.shape[-1]
    wb_w = nhb * HEAD_DIM
    wc_w = c_w0.shape[-1]
    lora_w, lora_a = c_wb.shape[1], c_ab.shape[1]
    o_q = 2 * wa_w
    o_k = o_q + wb_w
    o_v = o_k + wb_w
    o_f = o_v + wb_w
    o_c = o_f + nhb
    o_z = o_c + 3 * wc_w + lora_w + lora_a
    wi = w_in[l]
    row = lambda a: a.reshape(1, -1)
    head_lane = jnp.arange(wb_w) // HEAD_DIM
    w = dict(
        g=row(g_norm[l]),
        wa=wi[:, :o_q].astype(BF16), wq=wi[:, o_q:o_k].astype(BF16),
        wk=wi[:, o_k:o_v].astype(BF16), wv=wi[:, o_v:o_f].astype(BF16),
        wf=jnp.pad(wi[:, o_f:o_c], ((0, 0), (0, LANES - nhb))).astype(BF16),
        wc=wi[:, o_c:o_z].astype(BF16), wz=wi[:, o_z:].astype(BF16),
        wo=w_out[l].astype(BF16),
        ones_bd=(head_lane[:, None] == head_lane[None, :]).astype(BF16),
        qg_row=row(jnp.tile(b_qg[l], nhb)), kg_row=row(jnp.tile(b_kg[l], nhb)),
        kg_col=b_kg[l].reshape(HEAD_DIM, 1),
        fb=row(jnp.pad(b_fb[l], (0, LANES - nhb))),
        a_ln_g=row(a_ln_g[l]), a_ln_b=row(a_ln_b[l]), a_ws=a_ws[l],
        a_bs_exp=jnp.repeat(a_bs[l].T, HEAD_DIM, axis=1),
        c_mu=row(c_mu[l]), c_w0=row(c_w0[l]), c_wb=c_wb[l], c_a0=row(c_a0[l]), c_ab=c_ab[l],
        c_kk=row(c_kk[l]), c_ka=row(c_ka[l]), c_rk=row(c_rk[l]), c_gn_g=row(c_gn_g[l]), c_gn_b=row(c_gn_b[l]),
    )
    w["wk_t"] = w["wk"].T
    w["wv_t"] = w["wv"].T
    return w, nhb


def kernel(x_prompt, x_sample, c_prompt, c_sample, cache_fox_k, cache_fox_v, cache_fox_logf, page_table, state_wkv, state_shift, w_ada, b_ada, g_norm, w_in, w_out, a_ln_g, a_ln_b, a_ws, a_bs, b_qg, b_kg, b_fb, c_mu, c_w0, c_wb, c_a0, c_ab, c_kk, c_ka, c_rk, c_gn_g, c_gn_b):
    nl = w_in.shape[0]
    bp, tp, d = x_prompt.shape
    bs, ts, _ = x_sample.shape
    nhc = state_wkv.shape[2]
    n_pool, ps = cache_fox_k.shape[1], cache_fox_k.shape[2]

    n_c = bp + bs
    n_pad = -(-n_c // 8) * 8
    c_all = jnp.pad(jnp.concatenate([c_prompt, c_sample], axis=0), ((0, n_pad - n_c), (0, 0)))
    mod_all = _ada_mod(c_all, w_ada, b_ada)

    kt_pool = jnp.transpose(cache_fox_k, (0, 1, 3, 4, 2)).reshape(nl, n_pool, -1, ps)
    vt_pool = jnp.transpose(cache_fox_v, (0, 1, 3, 4, 2)).reshape(nl, n_pool, -1, ps)
    logf_t = jnp.transpose(cache_fox_logf, (0, 1, 3, 2))

    tq = min(512, tp)
    t_blk = min(256, tp)
    xp, xs = x_prompt, x_sample
    outs = {k: [] for k in ("kp", "vp", "fp", "ks", "vs", "fs", "cvs", "wp", "wsl", "sp", "ssl")}
    for l in range(nl):
        w, nhb = _layer_weights(l, w_in, w_out, g_norm, a_ln_g, a_ln_b, a_ws, a_bs, b_qg, b_kg, b_fb, c_mu,
                                c_w0, c_wb, c_a0, c_ab, c_kk, c_ka, c_rk, c_gn_g, c_gn_b)
        mod_p = mod_all[l, :bp][:, None, :]
        mod_s = mod_all[l, bp:n_c][:, None, :]

        wp_t = dict(w, wk=w["wk_t"], wv=w["wv_t"])
        pa, q, kt, vt, f, pc, z = _in_proj(xp, mod_p, wp_t, kv_t=True, nb_blk=1, t_blk=t_blk)
        ya, _ = _chunk_mix(pa, w)
        c, ct, ft = _cumsum_logf(f, nhb, tq)
        yb = _fox_prompt(q, kt, vt, c, ct, nhb, tq)
        yc, wkv_p = _rwkv(pc, jnp.zeros((bp, 1, pc.shape[-1]), F32), jnp.zeros((bp, nhc, HEAD_DIM, HEAD_DIM), F32),
                          w, HEAD_DIM)
        xp = _out_proj(ya, yb, yc, z, xp, mod_p, w["wo"], nb_blk=1, t_blk=t_blk)
        outs["kp"].append(jnp.transpose(kt.reshape(bp, nhb, HEAD_DIM, tp), (0, 3, 1, 2)))
        outs["vp"].append(jnp.transpose(vt.reshape(bp, nhb, HEAD_DIM, tp), (0, 3, 1, 2)))
        outs["fp"].append(jnp.transpose(ft, (0, 2, 1)))
        outs["wp"].append(wkv_p)
        outs["sp"].append(pc[:, -1, :])

        pa, q, kn, vn, fn, pc, z = _in_proj(xs, mod_s, w, kv_t=False, nb_blk=bs, t_blk=ts)
        ya, va = _chunk_mix(pa, w)
        cs = _cumsum_pages(page_table, logf_t, l)
        yb = _fox_sample(page_table, q, kn, vn, fn, cs, kt_pool, vt_pool, l, nhb)
        yc, wkv_s = _rwkv(pc, state_shift[l][:, None, :], state_wkv[l], w, HEAD_DIM)
        xs = _out_proj(ya, yb, yc, z, xs, mod_s, w["wo"], nb_blk=bs, t_blk=ts)
        outs["ks"].append(kn.reshape(bs, ts, nhb, HEAD_DIM))
        outs["vs"].append(vn.reshape(bs, ts, nhb, HEAD_DIM))
        outs["fs"].append(fn[:, :, :nhb])
        outs["cvs"].append(va.reshape(bs, ts, -1, HEAD_DIM))
        outs["wsl"].append(wkv_s)
        outs["ssl"].append(pc[:, -1, :])

    st = lambda k: jnp.stack(outs[k])
    return (xp, xs, st("kp"), st("vp"), st("fp"), st("ks"), st("vs"), st("fs"), st("cvs"),
            st("wp"), st("wsl"), st("sp"), st("ssl"))
```

```python
import functools
import math

import jax
import jax.numpy as jnp
from jax import lax
from jax.experimental import pallas as pl
from jax.experimental.pallas import tpu as pltpu

F32 = jnp.float32
BF16 = jnp.bfloat16
HEAD_DIM = 64
LANES = 128
NORM_EPS = 1e-6
LN_EPS = 1e-5
GN_EPS = 64e-5
DECAY_MAX = math.exp(-0.5)
LOG2E = math.log2(math.e)
BIAS_LANES = 4
NEG_BIG = -1e30
VMEM_LIMIT_BYTES = 56 * 1024 * 1024
HIGHEST = lax.Precision.HIGHEST


def _sigmoid(x):
    return 1.0 / (1.0 + jnp.exp(-x))


def _silu(x):
    return x * _sigmoid(x)


def _gelu_tanh(x):
    return 0.5 * x * (1.0 + jnp.tanh(math.sqrt(2.0 / math.pi) * (x + 0.044715 * (x * x * x))))


def _log_sigmoid(x):
    return jnp.minimum(x, 0.0) - jnp.log1p(jnp.exp(-jnp.abs(x)))


def _mm(a, b):
    return jnp.dot(a.astype(BF16), b.astype(BF16), preferred_element_type=F32)


def _mm_nt(a, b):
    return lax.dot_general(a.astype(BF16), b.astype(BF16), (((1,), (1,)), ((), ())),
                           preferred_element_type=F32)


def _mm_tn(a, b):
    return lax.dot_general(a.astype(BF16), b.astype(BF16), (((0,), (0,)), ((), ())),
                           preferred_element_type=F32)


def _split3(x):
    hi = x.astype(BF16)
    r1 = x - hi.astype(F32)
    mid = r1.astype(BF16)
    lo = (r1 - mid.astype(F32)).astype(BF16)
    return hi, mid, lo


def _mm_exact_l(m01, x):
    m = m01.astype(BF16)
    hi, mid, lo = _split3(x)
    d = lambda p: jnp.dot(m, p, preferred_element_type=F32)
    return d(hi) + d(mid) + d(lo)


def _mm_exact_r(x, m01):
    m = m01.astype(BF16)
    hi, mid, lo = _split3(x)
    d = lambda p: jnp.dot(p, m, preferred_element_type=F32)
    return d(hi) + d(mid) + d(lo)


def _iota(shape, dim):
    return lax.broadcasted_iota(jnp.int32, shape, dim)


def _cparams(sem):
    return pltpu.CompilerParams(dimension_semantics=sem, vmem_limit_bytes=VMEM_LIMIT_BYTES)


def _ada_kernel(c_ref, w_ref, b_ref, o_ref):
    c = c_ref[...]
    o_ref[...] = jnp.dot(_silu(c), w_ref[...], precision=HIGHEST, preferred_element_type=F32) + b_ref[...]


def _ada_mod(c_all, w_ada, b_ada):
    n, d = c_all.shape
    nl = w_ada.shape[0]
    return pl.pallas_call(
        _ada_kernel,
        grid=(nl, 3),
        in_specs=[pl.BlockSpec((n, d), lambda l, j: (0, 0)),
                  pl.BlockSpec((None, d, d), lambda l, j: (l, 0, j)),
                  pl.BlockSpec((None, 1, d), lambda l, j: (l, 0, j))],
        out_specs=pl.BlockSpec((None, n, d), lambda l, j: (l, 0, j)),
        out_shape=jax.ShapeDtypeStruct((nl, n, 3 * d), F32),
        compiler_params=_cparams(("arbitrary", "arbitrary")),
        name="ada_mod",
    )(c_all, w_ada, b_ada.reshape(nl, 1, 3 * d))


def _head_rms_rows(x, ones_bd, gain):
    xx = x * x
    hi = xx.astype(BF16)
    lo = (xx - hi.astype(F32)).astype(BF16)
    ss = jnp.dot(hi, ones_bd, preferred_element_type=F32) + jnp.dot(lo, ones_bd, preferred_element_type=F32)
    return x * lax.rsqrt(ss * (1.0 / HEAD_DIM) + NORM_EPS) * gain


def _in_kernel(x_ref, mod_ref, g_ref, wa_ref, wq_ref, wk_ref, wv_ref, wc_ref, wz_ref, wf_ref, ones_ref,
               qg_ref, kg_ref, fb_ref, *out_refs, kv_t):
    if kv_t:
        pa_ref, q_ref, k_ref, v_ref, kb_ref, vb_ref, f_ref, pc_ref, z_ref = out_refs
    else:
        pa_ref, q_ref, k_ref, v_ref, f_ref, pc_ref, z_ref = out_refs
    x = x_ref[...]
    nb, tt, d = x.shape
    m = nb * tt
    xn = x * lax.rsqrt(jnp.mean(x * x, axis=-1, keepdims=True) + NORM_EPS)
    mod = mod_ref[...]
    h = xn * g_ref[...] * (1.0 + mod[:, :, d:2 * d]) + mod[:, :, :d]
    hb = h.reshape(m, d).astype(BF16)

    def proj(w_ref):
        return jnp.dot(hb, w_ref[...], preferred_element_type=F32)

    def put(ref, val):
        ref[...] = val.reshape(nb, tt, val.shape[-1])

    put(pa_ref, proj(wa_ref))
    put(pc_ref, proj(wc_ref))
    put(z_ref, _silu(proj(wz_ref)))
    put(f_ref, _log_sigmoid(proj(wf_ref) + fb_ref[...]))
    if kv_t:
        def proj_t(w_ref):
            return lax.dot_general(w_ref[...], hb, (((1,), (1,)), ((), ())), preferred_element_type=F32)

        def head_rms_t(xt, gain_col):
            nh = xt.shape[0] // HEAD_DIM
            x3 = xt.reshape(nh, HEAD_DIM, m)
            ms = jnp.mean(x3 * x3, axis=1, keepdims=True)
            return (x3 * lax.rsqrt(ms + NORM_EPS) * gain_col[None]).reshape(nh * HEAD_DIM, m)

        qt = head_rms_t(proj_t(wq_ref), qg_ref[...]) * (HEAD_DIM ** -0.5 * LOG2E)
        nh = qt.shape[0] // HEAD_DIM
        first_half = _iota((LANES, 1), 0) < HEAD_DIM
        for h in range(nh):
            qp = qt[(h // 2) * LANES:(h // 2 + 1) * LANES, :]
            q_ref[0, h * LANES:(h + 1) * LANES, :] = jnp.where(first_half == (h % 2 == 0), qp, 0.0).astype(BF16)
        kt = head_rms_t(proj_t(wk_ref), kg_ref[...])
        vt = proj_t(wv_ref)
        k_ref[0] = kt
        v_ref[0] = vt
        kb_ref[0] = kt.T.astype(BF16)
        vb_ref[0] = vt.astype(BF16)
    else:
        qn = _head_rms_rows(proj(wq_ref), ones_ref[...], qg_ref[...])
        put(q_ref, qn * (HEAD_DIM ** -0.5))
        put(k_ref, _head_rms_rows(proj(wk_ref), ones_ref[...], kg_ref[...]))
        put(v_ref, proj(wv_ref))


def _in_proj(x, mod, w, *, kv_t, nb_blk, t_blk):
    b, t, d = x.shape
    wb = w["wq"].shape[1]
    grid = (b // nb_blk, t // t_blk)
    row = lambda width: pl.BlockSpec((nb_blk, t_blk, width), lambda i, j: (i, j, 0))
    full = lambda a: pl.BlockSpec(a.shape, lambda i, j: (0,) * a.ndim)
    wa, wq, wk, wv, wc, wz, wf = (w[n] for n in ("wa", "wq", "wk", "wv", "wc", "wz", "wf"))
    rows_out = lambda width, dt=F32: (jax.ShapeDtypeStruct((b, t, width), dt), row(width))
    if kv_t:
        assert nb_blk == 1
        wb = w["wq"].shape[0]
        t_out = lambda rows, dt: (jax.ShapeDtypeStruct((b, rows, t), dt),
                                  pl.BlockSpec((1, rows, t_blk), lambda i, j: (i, 0, j)))
        qg, kg = w["qg_col"], w["kg_col"]
        qkv = [t_out(2 * wb, BF16), t_out(wb, F32), t_out(wb, F32), rows_out(wb, BF16), t_out(wb, BF16)]
    else:
        qg, kg = w["qg_row"], w["kg_row"]
        qkv = [rows_out(wb), rows_out(wb), rows_out(wb)]
    outs = [rows_out(wa.shape[1])] + qkv + [rows_out(LANES), rows_out(wc.shape[1]), rows_out(wz.shape[1])]
    out_shape = [o[0] for o in outs]
    out_specs = [o[1] for o in outs]
    return pl.pallas_call(
        functools.partial(_in_kernel, kv_t=kv_t),
        grid=grid,
        in_specs=[row(d), pl.BlockSpec((nb_blk, 1, 3 * d), lambda i, j: (i, 0, 0)), full(w["g"]),
                  full(wa), full(wq), full(wk), full(wv), full(wc), full(wz), full(wf), full(w["ones_bd"]),
                  full(qg), full(kg), full(w["fb"])],
        out_specs=out_specs,
        out_shape=out_shape,
        compiler_params=_cparams(("arbitrary", "arbitrary")),
        name="in_proj_t" if kv_t else "in_proj",
    )(x, mod, w["g"], wa, wq, wk, wv, wc, wz, wf, w["ones_bd"], qg, kg, w["fb"])


def _mix_kernel(pa_ref, lng_ref, lnb_ref, ws_ref, bs_ref, ya_ref, va_ref, *, lowp):
    pa = pa_ref[0]
    tc = pa.shape[0]
    wa = pa.shape[1] // 2
    nh = wa // HEAD_DIM
    u = _gelu_tanh(pa[:, :wa])
    g = _gelu_tanh(pa[:, wa:])
    mu = jnp.mean(g, axis=-1, keepdims=True)
    var = jnp.mean(jnp.square(g - mu), axis=-1, keepdims=True)
    va = (g - mu) * lax.rsqrt(var + LN_EPS) * lng_ref[...] + lnb_ref[...]
    va_ref[0] = va
    causal = _iota((tc, tc), 1) <= _iota((tc, tc), 0)
    lane_head = _iota((1, wa), 1) // HEAD_DIM
    mix = jnp.zeros((tc, wa), F32)
    for h in range(nh):
        wc = jnp.where(causal, ws_ref[h, :tc, :tc], 0.0)
        vh = jnp.where(lane_head == h, va, 0.0)
        if lowp:
            mix = mix + _mm(wc, vh)
        else:
            mix = mix + jnp.dot(wc, vh, preferred_element_type=F32)
    ya_ref[0] = u * (mix + bs_ref[:tc, :])


def _chunk_mix(pa, w):
    b, t, w2 = pa.shape
    chunk = w["a_ws"].shape[-1]
    tc = min(chunk, t)
    assert t % tc == 0
    wa = w2 // 2
    full = lambda a: pl.BlockSpec(a.shape, lambda i, j: (0,) * a.ndim)
    return pl.pallas_call(
        functools.partial(_mix_kernel, lowp=tc >= 16),
        grid=(b, t // tc),
        in_specs=[pl.BlockSpec((1, tc, w2), lambda i, j: (i, j, 0)), full(w["a_ln_g"]), full(w["a_ln_b"]),
                  full(w["a_ws"]), full(w["a_bs_exp"])],
        out_specs=[pl.BlockSpec((1, tc, wa), lambda i, j: (i, j, 0))] * 2,
        out_shape=[jax.ShapeDtypeStruct((b, t, wa), F32)] * 2,
        compiler_params=_cparams(("arbitrary", "arbitrary")),
        name="chunk_mix",
    )(pa, w["a_ln_g"], w["a_ln_b"], w["a_ws"], w["a_bs_exp"])


def _cum_kernel(f_ref, ct_ref, ft_ref, ca_ref, carry, *, nh):
    @pl.when(pl.program_id(1) == 0)
    def _():
        carry[...] = jnp.zeros_like(carry)

    f = f_ref[0]
    tb = f.shape[0]
    tri = (_iota((tb, tb), 1) <= _iota((tb, tb), 0)).astype(F32)
    c = _mm_exact_l(tri, f) + carry[...]
    carry[...] = c[tb - 1:tb, :]
    c2 = c * LOG2E
    ct_ref[0] = c2.T[:nh, :]
    ft_ref[0] = f.T[:nh, :]
    src = _iota((LANES, LANES), 0)
    dst = _iota((LANES, LANES), 1)
    ca = jnp.zeros((tb, LANES), F32)
    for i, piece in enumerate(_split3(-c2)):
        place = ((dst == BIAS_LANES * src + i) & (src < nh)).astype(BF16)
        ca = ca + jnp.dot(piece, place, preferred_element_type=F32)
    ca_ref[0] = ca.astype(BF16)


def _cumsum_logf(f, nh, tb):
    b, t, _ = f.shape
    row = pl.BlockSpec((1, tb, LANES), lambda i, j: (i, j, 0))
    tr = pl.BlockSpec((1, nh, tb), lambda i, j: (i, 0, j))
    return pl.pallas_call(
        functools.partial(_cum_kernel, nh=nh),
        grid=(b, t // tb),
        in_specs=[row],
        out_specs=[tr, tr, row],
        out_shape=[jax.ShapeDtypeStruct((b, nh, t), F32), jax.ShapeDtypeStruct((b, nh, t), F32),
                   jax.ShapeDtypeStruct((b, t, LANES), BF16)],
        scratch_shapes=[pltpu.VMEM((1, LANES), F32)],
        compiler_params=_cparams(("arbitrary", "arbitrary")),
        name="cumsum_logf",
    )(f)


def _foxp_kernel(qt_ref, k_ref, vt_ref, ca_ref, ct_ref, o_ref, *scratch, nh, tq, rb, kb):
    m_scr, l_scr, acc_scr = scratch[:nh], scratch[nh:2 * nh], scratch[2 * nh:]
    qi = pl.program_id(1)
    ki = pl.program_id(2)

    @pl.when(ki == 0)
    def _():
        for h in range(nh):
            m_scr[h][...] = jnp.full_like(m_scr[h], NEG_BIG)
            l_scr[h][...] = jnp.zeros_like(l_scr[h])
            acc_scr[h][...] = jnp.zeros_like(acc_scr[h])

    sel_row = _iota((LANES, tq), 0)
    ones_rows = jnp.ones((16, tq), BF16)

    def block(masked):
        if masked:
            keep = _iota((tq, tq), 0) <= _iota((tq, tq), 1)
        for h in range(nh):
            pair = slice((h // 2) * LANES, (h // 2 + 1) * LANES)
            pick = ((sel_row >= BIAS_LANES * h) & (sel_row < BIAS_LANES * h + 3)).astype(BF16)
            k_cat = jnp.concatenate([k_ref[0, :, pair], ca_ref[0]], axis=1)
            q_cat = jnp.concatenate([qt_ref[0, h * LANES:(h + 1) * LANES, :], pick], axis=0)
            s = jnp.dot(k_cat, q_cat, preferred_element_type=F32)
            if masked:
                s = jnp.where(keep, s, NEG_BIG)
            cq = ct_ref[0, h:h + 1, :]
            m_run = m_scr[h][...]
            m_new = jnp.maximum(m_run, jnp.max(s, axis=0, keepdims=True) + cq)
            alpha = jnp.exp2(m_run - m_new)
            p = jnp.exp2(s - (m_new - cq)).astype(BF16)
            v_ext = jnp.concatenate([vt_ref[0, h * HEAD_DIM:(h + 1) * HEAD_DIM, :], ones_rows], axis=0)
            pv = jnp.dot(v_ext, p, preferred_element_type=F32)
            l_scr[h][...] = alpha * l_scr[h][...] + pv[HEAD_DIM:HEAD_DIM + 1, :]
            m_scr[h][...] = m_new
            acc_scr[h][...] = acc_scr[h][...] * alpha + pv[:HEAD_DIM, :]

    @pl.when(ki < qi)
    def _():
        block(False)

    @pl.when(ki == qi)
    def _():
        block(True)
        for pr in range(nh // 2):
            sl = slice(pr * LANES, (pr + 1) * LANES)
            o_pair = jnp.concatenate([acc_scr[h][...] / l_scr[h][...] for h in (2 * pr, 2 * pr + 1)], axis=0)
            o_ref[0, :, sl] = o_pair.T


def _fox_prompt(qt, kb16, vt16, ca, ct, nh, tq):
    b, _, t = qt.shape
    wb = nh * HEAD_DIM
    nq = t // tq
    rb = min(LANES, tq)
    kb = min(256, tq)
    q_t = lambda i, j, k: (i, 0, j)
    kv_t = lambda i, j, k: (i, 0, jnp.minimum(k, j))
    kv_r = lambda i, j, k: (i, jnp.minimum(k, j), 0)
    return pl.pallas_call(
        functools.partial(_foxp_kernel, nh=nh, tq=tq, rb=rb, kb=kb),
        grid=(b, nq, nq),
        in_specs=[pl.BlockSpec((1, nh * LANES, tq), q_t),
                  pl.BlockSpec((1, tq, wb), kv_r),
                  pl.BlockSpec((1, wb, tq), kv_t),
                  pl.BlockSpec((1, tq, LANES), kv_r),
                  pl.BlockSpec((1, nh, tq), q_t)],
        out_specs=pl.BlockSpec((1, tq, wb), lambda i, j, k: (i, j, 0)),
        out_shape=jax.ShapeDtypeStruct((b, t, wb), F32),
        scratch_shapes=([pltpu.VMEM((1, tq), F32)] * (2 * nh) + [pltpu.VMEM((HEAD_DIM, tq), F32)] * nh),
        compiler_params=_cparams(("arbitrary", "arbitrary", "arbitrary")),
        name="fox_prompt",
    )(qt, kb16, vt16, ca, ct)


def _cums_kernel(pt_ref, f_hbm, o_ref, buf, sem, *, layer, npg, nh):
    b = pl.program_id(0)

    def page_copy(j, page):
        return pltpu.make_async_copy(f_hbm.at[layer, page], buf.at[j], sem)

    def start(j, carry):
        page_copy(j, pt_ref[b, j]).start()
        return carry

    def wait(j, carry):
        page_copy(j, 0).wait()
        return carry

    lax.fori_loop(0, npg, start, 0)
    lax.fori_loop(0, npg, wait, 0)
    ps = buf.shape[-1]
    tri_keys = (_iota((ps, ps), 0) <= _iota((ps, ps), 1)).astype(F32)
    tri_pages = (_iota((npg, npg), 1) < _iota((npg, npg), 0)).astype(F32)
    for h in range(nh):
        xh = buf[:, h, :]
        y = _mm_exact_r(xh, tri_keys)
        tot = jnp.broadcast_to(y[:, ps - 1:ps], (npg, ps))
        o_ref[0, :, h, :] = y + _mm_exact_l(tri_pages, tot)


def _cumsum_pages(page_table, logf_t, layer):
    bs, npg = page_table.shape
    _, _, nh, ps = logf_t.shape
    return pl.pallas_call(
        functools.partial(_cums_kernel, layer=layer, npg=npg, nh=nh),
        grid_spec=pltpu.PrefetchScalarGridSpec(
            num_scalar_prefetch=1,
            grid=(bs,),
            in_specs=[pl.BlockSpec(memory_space=pl.ANY)],
            out_specs=pl.BlockSpec((1, npg, nh, ps), lambda i, pt: (i, 0, 0, 0)),
            scratch_shapes=[pltpu.VMEM((npg, nh, ps), F32), pltpu.SemaphoreType.DMA(())],
        ),
        out_shape=jax.ShapeDtypeStruct((bs, npg, nh, ps), F32),
        compiler_params=_cparams(("arbitrary",)),
        name="cumsum_pages",
    )(page_table, logf_t)


def _foxs_kernel(pt_ref, q_ref, kn_ref, vn_ref, fn_ref, cs_ref, cl_ref, *rest, nh, ts, pp):
    kt_refs = rest[:pp]
    vt_refs = rest[pp:2 * pp]
    o_ref = rest[2 * pp]
    m_scr, l_scr, acc_scr, cq_scr = rest[2 * pp + 1:]
    s_idx = pl.program_id(1)
    n_steps = pl.num_programs(1)
    nr = nh * ts
    wb = nh * HEAD_DIM
    ps = cs_ref.shape[-1]
    row_head = _iota((nr, 1), 0) // ts
    row_q = _iota((nr, 1), 0) % ts

    def rows_from_heads(x):
        return jnp.concatenate([jnp.broadcast_to(x[h:h + 1, :], (ts, x.shape[1])) for h in range(nh)], axis=0)

    q = q_ref[0]
    q_rep = jnp.concatenate([q] * nh, axis=0)
    qbd = jnp.where(_iota((nr, wb), 1) // HEAD_DIM == row_head, q_rep, 0.0).astype(BF16)

    fn = fn_ref[0]
    f_pad = jnp.concatenate([fn, jnp.zeros((LANES - ts, LANES), F32)], axis=0)
    a_new = rows_from_heads(f_pad.T[:nh, :])
    tri = (_iota((LANES, LANES), 0) <= _iota((LANES, LANES), 1)).astype(F32)
    c_new = _mm_exact_r(a_new, tri)

    @pl.when(s_idx == 0)
    def _():
        m_scr[...] = jnp.full_like(m_scr, NEG_BIG)
        l_scr[...] = jnp.zeros_like(l_scr)
        acc_scr[...] = jnp.zeros_like(acc_scr)
        total = rows_from_heads(cl_ref[0, 0][:, ps - 1:ps])
        own = jnp.sum(jnp.where(_iota((nr, LANES), 1) == row_q, c_new, 0.0), axis=-1, keepdims=True)
        cq_scr[...] = total + own

    cq = cq_scr[...]

    def online(s_list, v_list, v_is_t):
        m_prev = m_scr[...]
        m_new = m_prev
        for s in s_list:
            m_new = jnp.maximum(m_new, jnp.max(s, axis=-1, keepdims=True))
        alpha = jnp.exp(m_prev - m_new)
        l_add = jnp.zeros_like(m_prev)
        pv = jnp.zeros((nr, wb), F32)
        for s, v in zip(s_list, v_list):
            p = jnp.exp(s - m_new)
            l_add = l_add + jnp.sum(p, axis=-1, keepdims=True)
            if v_is_t:
                pv = pv + lax.dot_general(p.astype(BF16), v, (((1,), (1,)), ((), ())),
                                          preferred_element_type=F32)
            else:
                pv = pv + jnp.dot(p.astype(BF16), v, preferred_element_type=F32)
        m_scr[...] = m_new
        l_scr[...] = alpha * l_scr[...] + l_add
        acc_scr[...] = alpha * acc_scr[...] + pv

    s_list, v_list = [], []
    for j in range(pp):
        ck = rows_from_heads(cs_ref[0, j])
        s = jnp.dot(qbd, kt_refs[j][...].astype(BF16), preferred_element_type=F32) + cq - ck
        s_list.append(s)
        v_list.append(vt_refs[j][...].astype(BF16))
    online(s_list, v_list, True)

    @pl.when(s_idx == n_steps - 1)
    def _():
        kn = jnp.concatenate([kn_ref[0], jnp.zeros((LANES - ts, wb), F32)], axis=0).astype(BF16)
        vn = jnp.concatenate([vn_ref[0], jnp.zeros((LANES - ts, wb), F32)], axis=0).astype(BF16)
        total = rows_from_heads(cl_ref[0, 0][:, ps - 1:ps])
        s = lax.dot_general(qbd, kn, (((1,), (1,)), ((), ())), preferred_element_type=F32) + cq - (total + c_new)
        s = jnp.where(_iota((nr, LANES), 1) <= row_q, s, NEG_BIG)
        online([s], [vn], False)
        o_full = acc_scr[...] / l_scr[...]
        lane_head = _iota((1, wb), 1) // HEAD_DIM
        out = jnp.zeros((ts, wb), F32)
        for h in range(nh):
            out = out + jnp.where(lane_head == h, o_full[h * ts:(h + 1) * ts, :], 0.0)
        o_ref[0] = out


def _fox_sample(page_table, q, kn, vn, fn, cs, kt_pool, vt_pool, layer, nh):
    bs, ts, wb = q.shape
    npg = page_table.shape[1]
    ps = kt_pool.shape[-1]
    pp = min(16, npg)
    assert npg % pp == 0
    nr = nh * ts

    def page_spec(j):
        return pl.BlockSpec((None, None, wb, ps), lambda b, s, pt: (layer, pt[b, s * pp + j], 0, 0))

    row = lambda width: pl.BlockSpec((1, ts, width), lambda b, s, pt: (b, 0, 0))
    in_specs = [row(wb), row(wb), row(wb), row(LANES),
                pl.BlockSpec((1, pp, nh, ps), lambda b, s, pt: (b, s, 0, 0)),
                pl.BlockSpec((1, 1, nh, ps), lambda b, s, pt: (b, npg - 1, 0, 0))]
    in_specs += [page_spec(j) for j in range(pp)] * 2
    return pl.pallas_call(
        functools.partial(_foxs_kernel, nh=nh, ts=ts, pp=pp),
        grid_spec=pltpu.PrefetchScalarGridSpec(
            num_scalar_prefetch=1,
            grid=(bs, npg // pp),
            in_specs=in_specs,
            out_specs=row(wb),
            scratch_shapes=[pltpu.VMEM((nr, 1), F32), pltpu.VMEM((nr, 1), F32), pltpu.VMEM((nr, wb), F32),
                            pltpu.VMEM((nr, 1), F32)],
        ),
        out_shape=jax.ShapeDtypeStruct((bs, ts, wb), F32),
        compiler_params=_cparams(("arbitrary", "arbitrary")),
        name="fox_sample",
    )(page_table, q, kn, vn, fn, cs, cs, *([kt_pool] * pp), *([vt_pool] * pp))


def _seg_sum(x, ones_bd, lowp):
    if lowp:
        hi = x.astype(BF16)
        lo = (x - hi.astype(F32)).astype(BF16)
        return jnp.dot(hi, ones_bd, preferred_element_type=F32) + jnp.dot(lo, ones_bd, preferred_element_type=F32)
    return jnp.dot(x, ones_bd.astype(F32), precision=HIGHEST, preferred_element_type=F32)


def _rwkv_kernel(pc_ref, sh0_ref, s0_ref, mu_ref, w0_ref, wb_ref, a0_ref, ab_ref, kkw_ref, ka_ref, rk_ref,
                 gng_ref, gnb_ref, ones_ref, y_ref, sfin_ref, s_scr, prev_scr, *, nh, c, lowp):
    ti = pl.program_id(1)
    nseq, rows_seq, ws = pc_ref.shape
    n_chunks = rows_seq // c
    n = nseq * rows_seq
    wc = nh * HEAD_DIM
    lora = wb_ref.shape[0]

    @pl.when(ti == 0)
    def _():
        s_scr[...] = s0_ref[...]
        prev_scr[...] = sh0_ref[...]

    pc = pc_ref[...].reshape(n, ws)
    row = _iota((n, 1), 0)
    prev = pltpu.roll(pc, 1, axis=0)
    for sq in range(nseq):
        prev = jnp.where(row == sq * rows_seq, prev_scr[sq], prev)
        prev_scr[sq] = pc[(sq + 1) * rows_seq - 1:(sq + 1) * rows_seq, :]
    m = pc + (prev - pc) * mu_ref[...]
    r_all = m[:, :wc]
    k_raw = m[:, wc:2 * wc]
    v_all = m[:, 2 * wc:3 * wc]
    w_lo = m[:, 3 * wc:3 * wc + lora]
    a_lo = m[:, 3 * wc + lora:]
    hdot = lambda a, b: jnp.dot(a, b, precision=HIGHEST, preferred_element_type=F32)
    lw_all = -DECAY_MAX * _sigmoid(w0_ref[...] + hdot(jnp.tanh(w_lo), wb_ref[...]))
    a_all = _sigmoid(a0_ref[...] + hdot(a_lo, ab_ref[...]))
    kkr_all = k_raw * kkw_ref[...]
    k_all = k_raw * (1.0 + (a_all - 1.0) * ka_ref[...])
    seg = functools.partial(_seg_sum, ones_bd=ones_ref[...], lowp=lowp)
    kk_all = kkr_all * lax.rsqrt(jnp.maximum(seg(kkr_all * kkr_all), 1e-24))
    bonus_all = seg(r_all * k_all * rk_ref[...]) * v_all

    rr = _iota((n, n), 0)
    cc = _iota((n, n), 1)
    tri_chunks = ((cc <= rr) & (cc // c == rr // c)).astype(F32)
    if lowp:
        mm, mm_nt, mm_tn = _mm, _mm_nt, _mm_tn
        cum = _mm_exact_l(tri_chunks, lw_all)
    else:
        mm = lambda a, b: jnp.dot(a, b, preferred_element_type=F32)
        mm_nt = lambda a, b: lax.dot_general(a, b, (((1,), (1,)), ((), ())), preferred_element_type=F32)
        mm_tn = lambda a, b: lax.dot_general(a, b, (((0,), (0,)), ((), ())), preferred_element_type=F32)
        cum = hdot(tri_chunks, lw_all)
    e_pos = jnp.exp(cum)
    e_neg = jnp.exp(-cum)
    at_all = -kk_all * jnp.exp(cum - lw_all)
    bt_all = kk_all * a_all * e_neg
    kt_all = k_all * e_neg
    rt_all = r_all * e_pos

    def head_views(x):
        xr = pltpu.roll(x, HEAD_DIM, axis=1)
        views = []
        for h in range(nh):
            lane0 = (h // 2) * LANES if h % 2 == 0 else ((h // 2 + 1) * LANES) % wc
            views.append((x if h % 2 == 0 else xr)[:, lane0:lane0 + HEAD_DIM])
        return views

    at_v, bt_v, kt_v, rt_v, v_v, ep_v = (head_views(x) for x in (at_all, bt_all, kt_all, rt_all, v_all, e_pos))

    r2 = _iota((c, c), 0)
    c2 = _iota((c, c), 1)
    incl = c2 <= r2
    strict = c2 < r2
    eye = (c2 == r2).astype(F32)

    keys = [(sq, g, h) for sq in range(nseq) for g in range(n_chunks) for h in range(nh)]
    rows_of = lambda sq, g: slice(sq * rows_seq + g * c, sq * rows_seq + (g + 1) * c)
    at = {k: at_v[k[2]][rows_of(k[0], k[1]), :] for k in keys}
    bt = {k: bt_v[k[2]][rows_of(k[0], k[1]), :] for k in keys}
    kt = {k: kt_v[k[2]][rows_of(k[0], k[1]), :] for k in keys}
    rt = {k: rt_v[k[2]][rows_of(k[0], k[1]), :] for k in keys}
    vv = {k: v_v[k[2]][rows_of(k[0], k[1]), :] for k in keys}

    ar = {k: jnp.concatenate([at[k], rt[k]], axis=0) for k in keys}
    xb = {k: mm_nt(ar[k], bt[k]) for k in keys}
    xk = {k: mm_nt(ar[k], kt[k]) for k in keys}
    l_ab = {k: jnp.where(strict, xb[k][:c], 0.0) for k in keys}
    m_rb = {k: jnp.where(incl, xb[k][c:], 0.0) for k in keys}
    l_ak = {k: jnp.where(strict, xk[k][:c], 0.0) for k in keys}
    m_rk = {k: jnp.where(incl, xk[k][c:], 0.0) for k in keys}
    t_inv = {k: eye + l_ab[k] for k in keys}
    l_pow = l_ab
    span = 1
    while 2 * span < c:
        l_pow = {k: mm(l_pow[k], l_pow[k]) for k in keys}
        t_inv = {k: mm(t_inv[k], eye + l_pow[k]) for k in keys}
        span *= 2
    lv = {k: mm(l_ak[k], vv[k]) for k in keys}
    w_m = {k: mm(t_inv[k], at[k]) for k in keys}
    y_m = {k: mm(t_inv[k], lv[k]) for k in keys}
    wtb = {k: mm_tn(w_m[k], bt[k]) for k in keys}
    s_add = {k: mm_tn(y_m[k], bt[k]) + mm_tn(vv[k], kt[k]) for k in keys}
    rw = {k: rt[k] + mm(m_rb[k], w_m[k]) for k in keys}
    o_add = {k: mm(m_rb[k], y_m[k]) + mm(m_rk[k], vv[k]) for k in keys}

    o_heads = {}
    for sq in range(nseq):
        for h in range(nh):
            s = s_scr[sq, h]
            for g in range(n_chunks):
                k = (sq, g, h)
                o_heads[k] = mm_nt(rw[k], s) + o_add[k]
                last = sq * rows_seq + (g + 1) * c - 1
                s = (s + mm(s, wtb[k]) + s_add[k]) * ep_v[h][last:last + 1, :]
            s_scr[sq, h] = s

    o_all = jnp.concatenate(
        [jnp.concatenate([o_heads[(sq, g, h)] for h in range(nh)], axis=1)
         for sq in range(nseq) for g in range(n_chunks)], axis=0)
    mu_o = seg(o_all) * (1.0 / HEAD_DIM)
    dev = o_all - mu_o
    var = seg(dev * dev) * (1.0 / HEAD_DIM)
    y = dev * lax.rsqrt(var + GN_EPS) * gng_ref[...] + gnb_ref[...] + bonus_all
    y_ref[...] = y.reshape(nseq, rows_seq, wc)

    @pl.when(ti == pl.num_programs(1) - 1)
    def _():
        sfin_ref[...] = s_scr[...]


def _rwkv(pc, shift0, wkv0, w, *, c, n_chunks, nseq):
    b, t, ws = pc.shape
    nh = wkv0.shape[1]
    wc = nh * HEAD_DIM
    rows = c * n_chunks
    assert t % rows == 0 and b % nseq == 0 and nh % 2 == 0
    full = lambda a: pl.BlockSpec(a.shape, lambda i, j: (0,) * a.ndim)
    names = ("c_mu", "c_w0", "c_wb", "c_a0", "c_ab", "c_kk", "c_ka", "c_rk", "c_gn_g", "c_gn_b", "ones_c")
    params = [w[n] for n in names]
    state_spec = pl.BlockSpec((nseq, nh, HEAD_DIM, HEAD_DIM), lambda i, j: (i, 0, 0, 0))
    return pl.pallas_call(
        functools.partial(_rwkv_kernel, nh=nh, c=c, lowp=c >= 16),
        grid=(b // nseq, t // rows),
        in_specs=[pl.BlockSpec((nseq, rows, ws), lambda i, j: (i, j, 0)),
                  pl.BlockSpec((nseq, 1, ws), lambda i, j: (i, 0, 0)),
                  state_spec] + [full(p) for p in params],
        out_specs=[pl.BlockSpec((nseq, rows, wc), lambda i, j: (i, j, 0)), state_spec],
        out_shape=[jax.ShapeDtypeStruct((b, t, wc), F32), jax.ShapeDtypeStruct((b, nh, HEAD_DIM, HEAD_DIM), F32)],
        scratch_shapes=[pltpu.VMEM((nseq, nh, HEAD_DIM, HEAD_DIM), F32), pltpu.VMEM((nseq, 1, ws), F32)],
        compiler_params=_cparams(("arbitrary", "arbitrary")),
        name="rwkv7",
    )(pc, shift0, wkv0, *params)


def _out_kernel(ya_ref, yb_ref, yc_ref, z_ref, x_ref, mod_ref, wo_ref, o_ref):
    x = x_ref[...]
    nb, tt, d = x.shape
    m = nb * tt
    z = z_ref[...]
    wa = ya_ref.shape[-1]
    wb = yb_ref.shape[-1]
    ya = (ya_ref[...] * z[:, :, :wa]).reshape(m, wa).astype(BF16)
    yb = (yb_ref[...] * z[:, :, wa:wa + wb]).reshape(m, wb).astype(BF16)
    yc = (yc_ref[...] * z[:, :, wa + wb:]).reshape(m, -1).astype(BF16)
    y = (jnp.dot(ya, wo_ref[:wa, :], preferred_element_type=F32)
         + jnp.dot(yb, wo_ref[wa:wa + wb, :], preferred_element_type=F32)
         + jnp.dot(yc, wo_ref[wa + wb:, :], preferred_element_type=F32))
    o_ref[...] = x + mod_ref[...][:, :, 2 * d:] * y.reshape(nb, tt, d)


def _out_proj(ya, yb, yc, z, x, mod, wo, *, nb_blk, t_blk):
    b, t, d = x.shape
    row = lambda a: pl.BlockSpec((nb_blk, t_blk, a.shape[-1]), lambda i, j: (i, j, 0))
    return pl.pallas_call(
        _out_kernel,
        grid=(b // nb_blk, t // t_blk),
        in_specs=[row(ya), row(yb), row(yc), row(z), row(x),
                  pl.BlockSpec((nb_blk, 1, 3 * d), lambda i, j: (i, 0, 0)),
                  pl.BlockSpec(wo.shape, lambda i, j: (0, 0))],
        out_specs=row(x),
        out_shape=jax.ShapeDtypeStruct((b, t, d), F32),
        compiler_params=_cparams(("arbitrary", "arbitrary")),
        name="out_proj",
    )(ya, yb, yc, z, x, mod, wo)


def _layer_weights(l, w_in, w_out, g_norm, a_ln_g, a_ln_b, a_ws, a_bs, b_qg, b_kg, b_fb, c_mu, c_w0, c_wb,
                   c_a0, c_ab, c_kk, c_ka, c_rk, c_gn_g, c_gn_b):
    wa_w = a_ln_g.shape[-1]
    nhb = b_fb.shape[-1]
    wb_w = nhb * HEAD_DIM
    wc_w = c_w0.shape[-1]
    lora_w, lora_a = c_wb.shape[1], c_ab.shape[1]
    o_q = 2 * wa_w
    o_k = o_q + wb_w
    o_v = o_k + wb_w
    o_f = o_v + wb_w
    o_c = o_f + nhb
    o_z = o_c + 3 * wc_w + lora_w + lora_a
    wi = w_in[l]
    row = lambda a: a.reshape(1, -1)
    head_lane = jnp.arange(wb_w) // HEAD_DIM
    head_lane_c = jnp.arange(wc_w) // HEAD_DIM
    w = dict(
        g=row(g_norm[l]),
        wa=wi[:, :o_q].astype(BF16), wq=wi[:, o_q:o_k].astype(BF16),
        wk=wi[:, o_k:o_v].astype(BF16), wv=wi[:, o_v:o_f].astype(BF16),
        wf=jnp.pad(wi[:, o_f:o_c], ((0, 0), (0, LANES - nhb))).astype(BF16),
        wc=wi[:, o_c:o_z].astype(BF16), wz=wi[:, o_z:].astype(BF16),
        wo=w_out[l].astype(BF16),
        ones_bd=(head_lane[:, None] == head_lane[None, :]).astype(BF16),
        ones_c=(head_lane_c[:, None] == head_lane_c[None, :]).astype(BF16),
        qg_row=row(jnp.tile(b_qg[l], nhb)), kg_row=row(jnp.tile(b_kg[l], nhb)),
        qg_col=b_qg[l].reshape(HEAD_DIM, 1), kg_col=b_kg[l].reshape(HEAD_DIM, 1),
        fb=row(jnp.pad(b_fb[l], (0, LANES - nhb))),
        a_ln_g=row(a_ln_g[l]), a_ln_b=row(a_ln_b[l]), a_ws=a_ws[l],
        a_bs_exp=jnp.repeat(a_bs[l].T, HEAD_DIM, axis=1),
        c_mu=row(c_mu[l]), c_w0=row(c_w0[l]), c_wb=c_wb[l], c_a0=row(c_a0[l]), c_ab=c_ab[l],
        c_kk=row(c_kk[l]), c_ka=row(c_ka[l]), c_rk=row(c_rk[l]), c_gn_g=row(c_gn_g[l]), c_gn_b=row(c_gn_b[l]),
    )
    w["wq_t"] = w["wq"].T
    w["wk_t"] = w["wk"].T
    w["wv_t"] = w["wv"].T
    return w, nhb


def kernel(x_prompt, x_sample, c_prompt, c_sample, cache_fox_k, cache_fox_v, cache_fox_logf, page_table, state_wkv, state_shift, w_ada, b_ada, g_norm, w_in, w_out, a_ln_g, a_ln_b, a_ws, a_bs, b_qg, b_kg, b_fb, c_mu, c_w0, c_wb, c_a0, c_ab, c_kk, c_ka, c_rk, c_gn_g, c_gn_b):
    nl = w_in.shape[0]
    bp, tp, d = x_prompt.shape
    bs, ts, _ = x_sample.shape
    nhc = state_wkv.shape[2]
    n_pool, ps = cache_fox_k.shape[1], cache_fox_k.shape[2]

    n_c = bp + bs
    n_pad = -(-n_c // 8) * 8
    c_all = jnp.pad(jnp.concatenate([c_prompt, c_sample], axis=0), ((0, n_pad - n_c), (0, 0)))
    mod_all = _ada_mod(c_all, w_ada, b_ada)

    kt_pool = jnp.transpose(cache_fox_k, (0, 1, 3, 4, 2)).reshape(nl, n_pool, -1, ps)
    vt_pool = jnp.transpose(cache_fox_v, (0, 1, 3, 4, 2)).reshape(nl, n_pool, -1, ps)
    logf_t = jnp.transpose(cache_fox_logf, (0, 1, 3, 2))

    tq = min(512, tp)
    t_blk = min(256, tp)
    rw_chunk = min(HEAD_DIM, tp)
    rw_chunks_per_step = max(1, min(4, tp // rw_chunk))
    xp, xs = x_prompt, x_sample
    outs = {k: [] for k in ("kp", "vp", "fp", "ks", "vs", "fs", "cvs", "wp", "wsl", "sp", "ssl")}
    for l in range(nl):
        w, nhb = _layer_weights(l, w_in, w_out, g_norm, a_ln_g, a_ln_b, a_ws, a_bs, b_qg, b_kg, b_fb, c_mu,
                                c_w0, c_wb, c_a0, c_ab, c_kk, c_ka, c_rk, c_gn_g, c_gn_b)
        mod_p = mod_all[l, :bp][:, None, :]
        mod_s = mod_all[l, bp:n_c][:, None, :]

        wp_t = dict(w, wq=w["wq_t"], wk=w["wk_t"], wv=w["wv_t"])
        pa, qt, kt, vt, kb16, vt16, f, pc, z = _in_proj(xp, mod_p, wp_t, kv_t=True, nb_blk=1, t_blk=t_blk)
        ya, _ = _chunk_mix(pa, w)
        ct, ft, ca = _cumsum_logf(f, nhb, tq)
        yb = _fox_prompt(qt, kb16, vt16, ca, ct, nhb, tq)
        yc, wkv_p = _rwkv(pc, jnp.zeros((bp, 1, pc.shape[-1]), F32), jnp.zeros((bp, nhc, HEAD_DIM, HEAD_DIM), F32),
                          w, c=rw_chunk, n_chunks=rw_chunks_per_step, nseq=1)
        xp = _out_proj(ya, yb, yc, z, xp, mod_p, w["wo"], nb_blk=1, t_blk=t_blk)
        outs["kp"].append(jnp.transpose(kt.reshape(bp, nhb, HEAD_DIM, tp), (0, 3, 1, 2)))
        outs["vp"].append(jnp.transpose(vt.reshape(bp, nhb, HEAD_DIM, tp), (0, 3, 1, 2)))
        outs["fp"].append(jnp.transpose(ft, (0, 2, 1)))
        outs["wp"].append(wkv_p)
        outs["sp"].append(pc[:, -1, :])

        pa, q, kn, vn, fn, pc, z = _in_proj(xs, mod_s, w, kv_t=False, nb_blk=bs, t_blk=ts)
        ya, va = _chunk_mix(pa, w)
        cs = _cumsum_pages(page_table, logf_t, l)
        yb = _fox_sample(page_table, q, kn, vn, fn, cs, kt_pool, vt_pool, l, nhb)
        yc, wkv_s = _rwkv(pc, state_shift[l][:, None, :], state_wkv[l], w, c=ts, n_chunks=1, nseq=min(8, bs))
        xs = _out_proj(ya, yb, yc, z, xs, mod_s, w["wo"], nb_blk=bs, t_blk=ts)
        outs["ks"].append(kn.reshape(bs, ts, nhb, HEAD_DIM))
        outs["vs"].append(vn.reshape(bs, ts, nhb, HEAD_DIM))
        outs["fs"].append(fn[:, :, :nhb])
        outs["cvs"].append(va.reshape(bs, ts, -1, HEAD_DIM))
        outs["wsl"].append(wkv_s)
        outs["ssl"].append(pc[:, -1, :])

    st = lambda k: jnp.stack(outs[k])
    return (xp, xs, st("kp"), st("vp"), st("fp"), st("ks"), st("vs"), st("fs"), st("cvs"),
            st("wp"), st("wsl"), st("sp"), st("ssl"))
```

```python
import functools
import math

import jax
import jax.numpy as jnp
from jax import lax
from jax.experimental import pallas as pl
from jax.experimental.pallas import tpu as pltpu

F32 = jnp.float32
BF16 = jnp.bfloat16
HEAD_DIM = 64
LANES = 128
NORM_EPS = 1e-6
LN_EPS = 1e-5
GN_EPS = 64e-5
DECAY_MAX = math.exp(-0.5)
LOG2E = math.log2(math.e)
PAGES_PER_STEP = 32
BIAS_LANES = 4
NEG_BIG = -1e30
VMEM_LIMIT_BYTES = 56 * 1024 * 1024
HIGHEST = lax.Precision.HIGHEST


def _sigmoid(x):
    return 1.0 / (1.0 + jnp.exp(-x))


def _silu(x):
    return x * _sigmoid(x)


def _gelu_tanh(x):
    return 0.5 * x * (1.0 + jnp.tanh(math.sqrt(2.0 / math.pi) * (x + 0.044715 * (x * x * x))))


def _log_sigmoid(x):
    return jnp.minimum(x, 0.0) - jnp.log1p(jnp.exp(-jnp.abs(x)))


def _mm(a, b):
    return jnp.dot(a.astype(BF16), b.astype(BF16), preferred_element_type=F32)


def _mm_nt(a, b):
    return lax.dot_general(a.astype(BF16), b.astype(BF16), (((1,), (1,)), ((), ())),
                           preferred_element_type=F32)


def _mm_tn(a, b):
    return lax.dot_general(a.astype(BF16), b.astype(BF16), (((0,), (0,)), ((), ())),
                           preferred_element_type=F32)


def _split3(x):
    hi = x.astype(BF16)
    r1 = x - hi.astype(F32)
    mid = r1.astype(BF16)
    lo = (r1 - mid.astype(F32)).astype(BF16)
    return hi, mid, lo


def _mm_exact_l(m01, x):
    m = m01.astype(BF16)
    hi, mid, lo = _split3(x)
    d = lambda p: jnp.dot(m, p, preferred_element_type=F32)
    return d(hi) + d(mid) + d(lo)


def _mm_exact_r(x, m01):
    m = m01.astype(BF16)
    hi, mid, lo = _split3(x)
    d = lambda p: jnp.dot(p, m, preferred_element_type=F32)
    return d(hi) + d(mid) + d(lo)


def _iota(shape, dim):
    return lax.broadcasted_iota(jnp.int32, shape, dim)


def _cparams(sem):
    return pltpu.CompilerParams(dimension_semantics=sem, vmem_limit_bytes=VMEM_LIMIT_BYTES)


def _ada_kernel(c_ref, w_ref, b_ref, o_ref):
    c = c_ref[...]
    o_ref[...] = jnp.dot(_silu(c), w_ref[...], precision=HIGHEST, preferred_element_type=F32) + b_ref[...]


def _ada_mod(c_all, w_ada, b_ada):
    n, d = c_all.shape
    nl = w_ada.shape[0]
    return pl.pallas_call(
        _ada_kernel,
        grid=(nl, 3),
        in_specs=[pl.BlockSpec((n, d), lambda l, j: (0, 0)),
                  pl.BlockSpec((None, d, d), lambda l, j: (l, 0, j)),
                  pl.BlockSpec((None, 1, d), lambda l, j: (l, 0, j))],
        out_specs=pl.BlockSpec((None, n, d), lambda l, j: (l, 0, j)),
        out_shape=jax.ShapeDtypeStruct((nl, n, 3 * d), F32),
        compiler_params=_cparams(("arbitrary", "arbitrary")),
        name="ada_mod",
    )(c_all, w_ada, b_ada.reshape(nl, 1, 3 * d))


def _head_rms_rows(x, ones_bd, gain):
    xx = x * x
    hi = xx.astype(BF16)
    lo = (xx - hi.astype(F32)).astype(BF16)
    ss = jnp.dot(hi, ones_bd, preferred_element_type=F32) + jnp.dot(lo, ones_bd, preferred_element_type=F32)
    return x * lax.rsqrt(ss * (1.0 / HEAD_DIM) + NORM_EPS) * gain


def _in_kernel(x_ref, mod_ref, g_ref, wa_ref, wq_ref, wk_ref, wv_ref, wc_ref, wz_ref, wf_ref, ones_ref,
               qg_ref, kg_ref, fb_ref, *out_refs, kv_t):
    if kv_t:
        pa_ref, q_ref, k_ref, v_ref, kb_ref, vb_ref, f_ref, pc_ref, z_ref = out_refs
    else:
        pa_ref, q_ref, k_ref, v_ref, f_ref, pc_ref, z_ref = out_refs
    x = x_ref[...]
    nb, tt, d = x.shape
    m = nb * tt
    xn = x * lax.rsqrt(jnp.mean(x * x, axis=-1, keepdims=True) + NORM_EPS)
    mod = mod_ref[...]
    h = xn * g_ref[...] * (1.0 + mod[:, :, d:2 * d]) + mod[:, :, :d]
    hb = h.reshape(m, d).astype(BF16)

    def proj(w_ref):
        return jnp.dot(hb, w_ref[...], preferred_element_type=F32)

    def put(ref, val):
        ref[...] = val.reshape(nb, tt, val.shape[-1])

    put(pa_ref, proj(wa_ref))
    put(pc_ref, proj(wc_ref))
    put(z_ref, _silu(proj(wz_ref)))
    put(f_ref, _log_sigmoid(proj(wf_ref) + fb_ref[...]))
    if kv_t:
        def proj_t(w_ref):
            return lax.dot_general(w_ref[...], hb, (((1,), (1,)), ((), ())), preferred_element_type=F32)

        def head_rms_t(xt, gain_col):
            nh = xt.shape[0] // HEAD_DIM
            x3 = xt.reshape(nh, HEAD_DIM, m)
            ms = jnp.mean(x3 * x3, axis=1, keepdims=True)
            return (x3 * lax.rsqrt(ms + NORM_EPS) * gain_col[None]).reshape(nh * HEAD_DIM, m)

        qt = head_rms_t(proj_t(wq_ref), qg_ref[...]) * (HEAD_DIM ** -0.5 * LOG2E)
        nh = qt.shape[0] // HEAD_DIM
        first_half = _iota((LANES, 1), 0) < HEAD_DIM
        for h in range(nh):
            qp = qt[(h // 2) * LANES:(h // 2 + 1) * LANES, :]
            q_ref[0, h * LANES:(h + 1) * LANES, :] = jnp.where(first_half == (h % 2 == 0), qp, 0.0).astype(BF16)
        kt = head_rms_t(proj_t(wk_ref), kg_ref[...])
        vt = proj_t(wv_ref)
        k_ref[0] = kt
        v_ref[0] = vt
        kb_ref[0] = kt.T.astype(BF16)
        vb_ref[0] = vt.astype(BF16)
    else:
        qn = _head_rms_rows(proj(wq_ref), ones_ref[...], qg_ref[...])
        put(q_ref, qn * (HEAD_DIM ** -0.5))
        put(k_ref, _head_rms_rows(proj(wk_ref), ones_ref[...], kg_ref[...]))
        put(v_ref, proj(wv_ref))


def _in_proj(x, mod, w, *, kv_t, nb_blk, t_blk):
    b, t, d = x.shape
    wb = w["wq"].shape[1]
    grid = (b // nb_blk, t // t_blk)
    row = lambda width: pl.BlockSpec((nb_blk, t_blk, width), lambda i, j: (i, j, 0))
    full = lambda a: pl.BlockSpec(a.shape, lambda i, j: (0,) * a.ndim)
    wa, wq, wk, wv, wc, wz, wf = (w[n] for n in ("wa", "wq", "wk", "wv", "wc", "wz", "wf"))
    rows_out = lambda width, dt=F32: (jax.ShapeDtypeStruct((b, t, width), dt), row(width))
    if kv_t:
        assert nb_blk == 1
        wb = w["wq"].shape[0]
        t_out = lambda rows, dt: (jax.ShapeDtypeStruct((b, rows, t), dt),
                                  pl.BlockSpec((1, rows, t_blk), lambda i, j: (i, 0, j)))
        qg, kg = w["qg_col"], w["kg_col"]
        qkv = [t_out(2 * wb, BF16), t_out(wb, F32), t_out(wb, F32), rows_out(wb, BF16), t_out(wb, BF16)]
    else:
        qg, kg = w["qg_row"], w["kg_row"]
        qkv = [rows_out(wb), rows_out(wb), rows_out(wb)]
    outs = [rows_out(wa.shape[1])] + qkv + [rows_out(LANES), rows_out(wc.shape[1]), rows_out(wz.shape[1])]
    out_shape = [o[0] for o in outs]
    out_specs = [o[1] for o in outs]
    return pl.pallas_call(
        functools.partial(_in_kernel, kv_t=kv_t),
        grid=grid,
        in_specs=[row(d), pl.BlockSpec((nb_blk, 1, 3 * d), lambda i, j: (i, 0, 0)), full(w["g"]),
                  full(wa), full(wq), full(wk), full(wv), full(wc), full(wz), full(wf), full(w["ones_bd"]),
                  full(qg), full(kg), full(w["fb"])],
        out_specs=out_specs,
        out_shape=out_shape,
        compiler_params=_cparams(("arbitrary", "arbitrary")),
        name="in_proj_t" if kv_t else "in_proj",
    )(x, mod, w["g"], wa, wq, wk, wv, wc, wz, wf, w["ones_bd"], qg, kg, w["fb"])


def _mix_kernel(pa_ref, lng_ref, lnb_ref, ws_ref, bs_ref, ya_ref, *va_refs, lowp):
    pa = pa_ref[0]
    tc = pa.shape[0]
    wa = pa.shape[1] // 2
    nh = wa // HEAD_DIM
    u = _gelu_tanh(pa[:, :wa])
    g = _gelu_tanh(pa[:, wa:])
    mu = jnp.mean(g, axis=-1, keepdims=True)
    var = jnp.mean(jnp.square(g - mu), axis=-1, keepdims=True)
    va = (g - mu) * lax.rsqrt(var + LN_EPS) * lng_ref[...] + lnb_ref[...]
    for va_ref in va_refs:
        va_ref[0] = va
    causal = _iota((tc, tc), 1) <= _iota((tc, tc), 0)
    lane_head = _iota((1, wa), 1) // HEAD_DIM
    mix = jnp.zeros((tc, wa), F32)
    for h in range(nh):
        wc = jnp.where(causal, ws_ref[h, :tc, :tc], 0.0)
        vh = jnp.where(lane_head == h, va, 0.0)
        if lowp:
            mix = mix + _mm(wc, vh)
        else:
            mix = mix + jnp.dot(wc, vh, preferred_element_type=F32)
    ya_ref[0] = u * (mix + bs_ref[:tc, :])


def _chunk_mix(pa, w, *, with_va):
    n_out = 2 if with_va else 1
    b, t, w2 = pa.shape
    chunk = w["a_ws"].shape[-1]
    tc = min(chunk, t)
    assert t % tc == 0
    wa = w2 // 2
    full = lambda a: pl.BlockSpec(a.shape, lambda i, j: (0,) * a.ndim)
    return pl.pallas_call(
        functools.partial(_mix_kernel, lowp=tc >= 16),
        grid=(b, t // tc),
        in_specs=[pl.BlockSpec((1, tc, w2), lambda i, j: (i, j, 0)), full(w["a_ln_g"]), full(w["a_ln_b"]),
                  full(w["a_ws"]), full(w["a_bs_exp"])],
        out_specs=[pl.BlockSpec((1, tc, wa), lambda i, j: (i, j, 0))] * n_out,
        out_shape=[jax.ShapeDtypeStruct((b, t, wa), F32)] * n_out,
        compiler_params=_cparams(("arbitrary", "arbitrary")),
        name="chunk_mix",
    )(pa, w["a_ln_g"], w["a_ln_b"], w["a_ws"], w["a_bs_exp"])


def _cum_kernel(f_ref, ct_ref, ft_ref, ca_ref, carry, *, nh):
    @pl.when(pl.program_id(1) == 0)
    def _():
        carry[...] = jnp.zeros_like(carry)

    f = f_ref[0]
    tb = f.shape[0]
    tri = (_iota((tb, tb), 1) <= _iota((tb, tb), 0)).astype(F32)
    c = _mm_exact_l(tri, f) + carry[...]
    carry[...] = c[tb - 1:tb, :]
    c2 = c * LOG2E
    ct_ref[0] = c2.T[:nh, :]
    ft_ref[0] = f.T[:nh, :]
    src = _iota((LANES, LANES), 0)
    dst = _iota((LANES, LANES), 1)
    ca = jnp.zeros((tb, LANES), F32)
    for i, piece in enumerate(_split3(-c2)):
        place = ((dst == BIAS_LANES * src + i) & (src < nh)).astype(BF16)
        ca = ca + jnp.dot(piece, place, preferred_element_type=F32)
    ca_ref[0] = ca.astype(BF16)


def _cumsum_logf(f, nh, tb):
    b, t, _ = f.shape
    row = pl.BlockSpec((1, tb, LANES), lambda i, j: (i, j, 0))
    tr = pl.BlockSpec((1, nh, tb), lambda i, j: (i, 0, j))
    return pl.pallas_call(
        functools.partial(_cum_kernel, nh=nh),
        grid=(b, t // tb),
        in_specs=[row],
        out_specs=[tr, tr, row],
        out_shape=[jax.ShapeDtypeStruct((b, nh, t), F32), jax.ShapeDtypeStruct((b, nh, t), F32),
                   jax.ShapeDtypeStruct((b, t, LANES), BF16)],
        scratch_shapes=[pltpu.VMEM((1, LANES), F32)],
        compiler_params=_cparams(("arbitrary", "arbitrary")),
        name="cumsum_logf",
    )(f)


def _foxp_kernel(qt_ref, k_ref, vt_ref, ca_ref, ct_ref, o_ref, *scratch, nh, tq, rb, kb):
    m_scr, l_scr, acc_scr = scratch[:nh], scratch[nh:2 * nh], scratch[2 * nh:]
    qi = pl.program_id(1)
    ki = pl.program_id(2)

    @pl.when(ki == 0)
    def _():
        for h in range(nh):
            m_scr[h][...] = jnp.full_like(m_scr[h], NEG_BIG)
            l_scr[h][...] = jnp.zeros_like(l_scr[h])
            acc_scr[h][...] = jnp.zeros_like(acc_scr[h])

    sel_row = _iota((LANES, tq), 0)
    ones_rows = jnp.ones((16, tq), BF16)

    def block(masked):
        if masked:
            keep = _iota((tq, tq), 0) <= _iota((tq, tq), 1)
        def scores(h):
            pair = slice((h // 2) * LANES, (h // 2 + 1) * LANES)
            pick = ((sel_row >= BIAS_LANES * h) & (sel_row < BIAS_LANES * h + 3)).astype(BF16)
            k_cat = jnp.concatenate([k_ref[0, :, pair], ca_ref[0]], axis=1)
            q_cat = jnp.concatenate([qt_ref[0, h * LANES:(h + 1) * LANES, :], pick], axis=0)
            return jnp.dot(k_cat, q_cat, preferred_element_type=F32)

        s_next = scores(0)
        for h in range(nh):
            s = s_next
            if h + 1 < nh:
                s_next = scores(h + 1)
            if masked:
                s = jnp.where(keep, s, NEG_BIG)
            cq = ct_ref[0, h:h + 1, :]
            m_run = m_scr[h][...]
            m_new = jnp.maximum(m_run, jnp.max(s, axis=0, keepdims=True) + cq)
            alpha = jnp.exp2(m_run - m_new)
            p = jnp.exp2(s - (m_new - cq)).astype(BF16)
            v_ext = jnp.concatenate([vt_ref[0, h * HEAD_DIM:(h + 1) * HEAD_DIM, :], ones_rows], axis=0)
            pv = jnp.dot(v_ext, p, preferred_element_type=F32)
            l_scr[h][...] = alpha * l_scr[h][...] + pv[HEAD_DIM:HEAD_DIM + 1, :]
            m_scr[h][...] = m_new
            acc_scr[h][...] = acc_scr[h][...] * alpha + pv[:HEAD_DIM, :]

    @pl.when(ki < qi)
    def _():
        block(False)

    @pl.when(ki == qi)
    def _():
        block(True)
        for pr in range(nh // 2):
            sl = slice(pr * LANES, (pr + 1) * LANES)
            o_pair = jnp.concatenate([acc_scr[h][...] / l_scr[h][...] for h in (2 * pr, 2 * pr + 1)], axis=0)
            o_ref[0, :, sl] = o_pair.T


def _fox_prompt(qt, kb16, vt16, ca, ct, nh, tq):
    b, _, t = qt.shape
    wb = nh * HEAD_DIM
    nq = t // tq
    rb = min(LANES, tq)
    kb = min(256, tq)
    q_t = lambda i, j, k: (i, 0, j)
    kv_t = lambda i, j, k: (i, 0, jnp.minimum(k, j))
    kv_r = lambda i, j, k: (i, jnp.minimum(k, j), 0)
    return pl.pallas_call(
        functools.partial(_foxp_kernel, nh=nh, tq=tq, rb=rb, kb=kb),
        grid=(b, nq, nq),
        in_specs=[pl.BlockSpec((1, nh * LANES, tq), q_t),
                  pl.BlockSpec((1, tq, wb), kv_r),
                  pl.BlockSpec((1, wb, tq), kv_t),
                  pl.BlockSpec((1, tq, LANES), kv_r),
                  pl.BlockSpec((1, nh, tq), q_t)],
        out_specs=pl.BlockSpec((1, tq, wb), lambda i, j, k: (i, j, 0)),
        out_shape=jax.ShapeDtypeStruct((b, t, wb), F32),
        scratch_shapes=([pltpu.VMEM((1, tq), F32)] * (2 * nh) + [pltpu.VMEM((HEAD_DIM, tq), F32)] * nh),
        compiler_params=_cparams(("arbitrary", "arbitrary", "arbitrary")),
        name="fox_prompt",
    )(qt, kb16, vt16, ca, ct)


def _cums_kernel(pt_ref, f_hbm, o_ref, buf, sems, *, layer, npg, nh):
    b = pl.program_id(0)
    slot = b % 2

    def page_copy(seq, j, sl):
        return pltpu.make_async_copy(f_hbm.at[layer, pt_ref[seq, j]], buf.at[sl, j], sems.at[sl])

    def start_all(seq, sl):
        def body(j, carry):
            page_copy(seq, j, sl).start()
            return carry
        lax.fori_loop(0, npg, body, 0)

    def wait_all(seq, sl):
        def body(j, carry):
            page_copy(seq, j, sl).wait()
            return carry
        lax.fori_loop(0, npg, body, 0)

    @pl.when(b == 0)
    def _():
        start_all(b, slot)

    @pl.when(b + 1 < pl.num_programs(0))
    def _():
        start_all(b + 1, 1 - slot)

    wait_all(b, slot)
    ps = buf.shape[-1]
    tri_keys = (_iota((ps, ps), 0) <= _iota((ps, ps), 1)).astype(F32)
    tri_pages = (_iota((npg, npg), 1) < _iota((npg, npg), 0)).astype(F32)
    for h in range(nh):
        xh = buf[slot, :, h, :]
        y = _mm_exact_r(xh, tri_keys)
        tot = jnp.broadcast_to(y[:, ps - 1:ps], (npg, ps))
        o_ref[0, :, h, :] = y + _mm_exact_l(tri_pages, tot)


def _cumsum_pages(page_table, logf_t, layer):
    bs, npg = page_table.shape
    _, _, nh, ps = logf_t.shape
    return pl.pallas_call(
        functools.partial(_cums_kernel, layer=layer, npg=npg, nh=nh),
        grid_spec=pltpu.PrefetchScalarGridSpec(
            num_scalar_prefetch=1,
            grid=(bs,),
            in_specs=[pl.BlockSpec(memory_space=pl.ANY)],
            out_specs=pl.BlockSpec((1, npg, nh, ps), lambda i, pt: (i, 0, 0, 0)),
            scratch_shapes=[pltpu.VMEM((2, npg, nh, ps), F32), pltpu.SemaphoreType.DMA((2,))],
        ),
        out_shape=jax.ShapeDtypeStruct((bs, npg, nh, ps), F32),
        compiler_params=_cparams(("arbitrary",)),
        name="cumsum_pages",
    )(page_table, logf_t)


def _foxs_kernel(pt_ref, q_ref, kn_ref, vn_ref, fn_ref, cs_ref, cl_ref, *rest, nh, ts, pp):
    kt_refs = rest[:pp]
    vt_refs = rest[pp:2 * pp]
    o_ref = rest[2 * pp]
    m_scr, l_scr, acc_scr, cq_scr = rest[2 * pp + 1:]
    s_idx = pl.program_id(1)
    n_steps = pl.num_programs(1)
    nr = nh * ts
    wb = nh * HEAD_DIM
    ps = cs_ref.shape[-1]
    row_head = _iota((nr, 1), 0) // ts
    row_q = _iota((nr, 1), 0) % ts

    def rows_from_heads(x):
        return jnp.concatenate([jnp.broadcast_to(x[h:h + 1, :], (ts, x.shape[1])) for h in range(nh)], axis=0)

    q = q_ref[0]
    q_rep = jnp.concatenate([q] * nh, axis=0)
    qbd = jnp.where(_iota((nr, wb), 1) // HEAD_DIM == row_head, q_rep, 0.0).astype(BF16)

    fn = fn_ref[0]
    f_pad = jnp.concatenate([fn, jnp.zeros((LANES - ts, LANES), F32)], axis=0)
    a_new = rows_from_heads(f_pad.T[:nh, :])
    tri = (_iota((LANES, LANES), 0) <= _iota((LANES, LANES), 1)).astype(F32)
    c_new = _mm_exact_r(a_new, tri)

    @pl.when(s_idx == 0)
    def _():
        m_scr[...] = jnp.full_like(m_scr, NEG_BIG)
        l_scr[...] = jnp.zeros_like(l_scr)
        acc_scr[...] = jnp.zeros_like(acc_scr)
        total = rows_from_heads(cl_ref[0, 0][:, ps - 1:ps])
        own = jnp.sum(jnp.where(_iota((nr, LANES), 1) == row_q, c_new, 0.0), axis=-1, keepdims=True)
        cq_scr[...] = total + own

    cq = cq_scr[...]

    def online(s_list, v_list, v_is_t):
        m_prev = m_scr[...]
        m_new = m_prev
        for s in s_list:
            m_new = jnp.maximum(m_new, jnp.max(s, axis=-1, keepdims=True))
        alpha = jnp.exp(m_prev - m_new)
        l_add = jnp.zeros_like(m_prev)
        pv = jnp.zeros((nr, wb), F32)
        for s, v in zip(s_list, v_list):
            p = jnp.exp(s - m_new)
            l_add = l_add + jnp.sum(p, axis=-1, keepdims=True)
            if v_is_t:
                pv = pv + lax.dot_general(p.astype(BF16), v, (((1,), (1,)), ((), ())),
                                          preferred_element_type=F32)
            else:
                pv = pv + jnp.dot(p.astype(BF16), v, preferred_element_type=F32)
        m_scr[...] = m_new
        l_scr[...] = alpha * l_scr[...] + l_add
        acc_scr[...] = alpha * acc_scr[...] + pv

    s_list, v_list = [], []
    for j in range(pp):
        ck = rows_from_heads(cs_ref[0, j])
        s = jnp.dot(qbd, kt_refs[j][...].astype(BF16), preferred_element_type=F32) + cq - ck
        s_list.append(s)
        v_list.append(vt_refs[j][...].astype(BF16))
    online(s_list, v_list, True)

    @pl.when(s_idx == n_steps - 1)
    def _():
        kn = jnp.concatenate([kn_ref[0], jnp.zeros((LANES - ts, wb), F32)], axis=0).astype(BF16)
        vn = jnp.concatenate([vn_ref[0], jnp.zeros((LANES - ts, wb), F32)], axis=0).astype(BF16)
        total = rows_from_heads(cl_ref[0, 0][:, ps - 1:ps])
        s = lax.dot_general(qbd, kn, (((1,), (1,)), ((), ())), preferred_element_type=F32) + cq - (total + c_new)
        s = jnp.where(_iota((nr, LANES), 1) <= row_q, s, NEG_BIG)
        online([s], [vn], False)
        o_full = acc_scr[...] / l_scr[...]
        lane_head = _iota((1, wb), 1) // HEAD_DIM
        out = jnp.zeros((ts, wb), F32)
        for h in range(nh):
            out = out + jnp.where(lane_head == h, o_full[h * ts:(h + 1) * ts, :], 0.0)
        o_ref[0] = out


def _fox_sample(page_table, q, kn, vn, fn, cs, kt_pool, vt_pool, layer, nh):
    bs, ts, wb = q.shape
    npg = page_table.shape[1]
    ps = kt_pool.shape[-1]
    pp = min(PAGES_PER_STEP, npg)
    assert npg % pp == 0
    nr = nh * ts

    def page_spec(j):
        return pl.BlockSpec((None, None, wb, ps), lambda b, s, pt: (layer, pt[b, s * pp + j], 0, 0))

    row = lambda width: pl.BlockSpec((1, ts, width), lambda b, s, pt: (b, 0, 0))
    in_specs = [row(wb), row(wb), row(wb), row(LANES),
                pl.BlockSpec((1, pp, nh, ps), lambda b, s, pt: (b, s, 0, 0)),
                pl.BlockSpec((1, 1, nh, ps), lambda b, s, pt: (b, npg - 1, 0, 0))]
    in_specs += [page_spec(j) for j in range(pp)] * 2
    return pl.pallas_call(
        functools.partial(_foxs_kernel, nh=nh, ts=ts, pp=pp),
        grid_spec=pltpu.PrefetchScalarGridSpec(
            num_scalar_prefetch=1,
            grid=(bs, npg // pp),
            in_specs=in_specs,
            out_specs=row(wb),
            scratch_shapes=[pltpu.VMEM((nr, 1), F32), pltpu.VMEM((nr, 1), F32), pltpu.VMEM((nr, wb), F32),
                            pltpu.VMEM((nr, 1), F32)],
        ),
        out_shape=jax.ShapeDtypeStruct((bs, ts, wb), F32),
        compiler_params=_cparams(("arbitrary", "arbitrary")),
        name="fox_sample",
    )(page_table, q, kn, vn, fn, cs, cs, *([kt_pool] * pp), *([vt_pool] * pp))


def _seg_sum(x, ones_bd, lowp):
    if lowp:
        hi = x.astype(BF16)
        lo = (x - hi.astype(F32)).astype(BF16)
        return jnp.dot(hi, ones_bd, preferred_element_type=F32) + jnp.dot(lo, ones_bd, preferred_element_type=F32)
    return jnp.dot(x, ones_bd.astype(F32), precision=HIGHEST, preferred_element_type=F32)


def _rwkv_kernel(pc_ref, sh0_ref, s0_ref, mu_ref, w0_ref, wb_ref, a0_ref, ab_ref, kkw_ref, ka_ref, rk_ref,
                 gng_ref, gnb_ref, ones_ref, y_ref, sfin_ref, s_scr, prev_scr, *, nh, c, lowp):
    ti = pl.program_id(1)
    nseq, rows_seq, ws = pc_ref.shape
    n_chunks = rows_seq // c
    n = nseq * rows_seq
    wc = nh * HEAD_DIM
    lora = wb_ref.shape[0]

    @pl.when(ti == 0)
    def _():
        s_scr[...] = s0_ref[...]
        prev_scr[...] = sh0_ref[...]

    pc = pc_ref[...].reshape(n, ws)
    row = _iota((n, 1), 0)
    prev = pltpu.roll(pc, 1, axis=0)
    for sq in range(nseq):
        prev = jnp.where(row == sq * rows_seq, prev_scr[sq], prev)
        prev_scr[sq] = pc[(sq + 1) * rows_seq - 1:(sq + 1) * rows_seq, :]
    m = pc + (prev - pc) * mu_ref[...]
    r_all = m[:, :wc]
    k_raw = m[:, wc:2 * wc]
    v_all = m[:, 2 * wc:3 * wc]
    w_lo = m[:, 3 * wc:3 * wc + lora]
    a_lo = m[:, 3 * wc + lora:]
    hdot = lambda a, b: jnp.dot(a, b, precision=HIGHEST, preferred_element_type=F32)
    lw_all = -DECAY_MAX * _sigmoid(w0_ref[...] + hdot(jnp.tanh(w_lo), wb_ref[...]))
    a_all = _sigmoid(a0_ref[...] + hdot(a_lo, ab_ref[...]))
    kkr_all = k_raw * kkw_ref[...]
    k_all = k_raw * (1.0 + (a_all - 1.0) * ka_ref[...])
    seg = functools.partial(_seg_sum, ones_bd=ones_ref[...], lowp=lowp)
    kk_all = kkr_all * lax.rsqrt(jnp.maximum(seg(kkr_all * kkr_all), 1e-24))
    bonus_all = seg(r_all * k_all * rk_ref[...]) * v_all

    rr = _iota((n, n), 0)
    cc = _iota((n, n), 1)
    tri_chunks = ((cc <= rr) & (cc // c == rr // c)).astype(F32)
    if lowp:
        mm, mm_nt, mm_tn = _mm, _mm_nt, _mm_tn
        cum = _mm_exact_l(tri_chunks, lw_all)
    else:
        mm = lambda a, b: jnp.dot(a, b, preferred_element_type=F32)
        mm_nt = lambda a, b: lax.dot_general(a, b, (((1,), (1,)), ((), ())), preferred_element_type=F32)
        mm_tn = lambda a, b: lax.dot_general(a, b, (((0,), (0,)), ((), ())), preferred_element_type=F32)
        cum = hdot(tri_chunks, lw_all)
    e_pos = jnp.exp(cum)
    e_neg = jnp.exp(-cum)
    at_all = -kk_all * jnp.exp(cum - lw_all)
    bt_all = kk_all * a_all * e_neg
    kt_all = k_all * e_neg
    rt_all = r_all * e_pos

    def head_views(x):
        xr = pltpu.roll(x, HEAD_DIM, axis=1)
        views = []
        for h in range(nh):
            lane0 = (h // 2) * LANES if h % 2 == 0 else ((h // 2 + 1) * LANES) % wc
            views.append((x if h % 2 == 0 else xr)[:, lane0:lane0 + HEAD_DIM])
        return views

    at_v, bt_v, kt_v, rt_v, v_v, ep_v = (head_views(x) for x in (at_all, bt_all, kt_all, rt_all, v_all, e_pos))

    r2 = _iota((c, c), 0)
    c2 = _iota((c, c), 1)
    incl = c2 <= r2
    strict = c2 < r2
    eye = (c2 == r2).astype(F32)

    keys = [(sq, g, h) for sq in range(nseq) for g in range(n_chunks) for h in range(nh)]
    rows_of = lambda sq, g: slice(sq * rows_seq + g * c, sq * rows_seq + (g + 1) * c)
    at = {k: at_v[k[2]][rows_of(k[0], k[1]), :] for k in keys}
    bt = {k: bt_v[k[2]][rows_of(k[0], k[1]), :] for k in keys}
    kt = {k: kt_v[k[2]][rows_of(k[0], k[1]), :] for k in keys}
    rt = {k: rt_v[k[2]][rows_of(k[0], k[1]), :] for k in keys}
    vv = {k: v_v[k[2]][rows_of(k[0], k[1]), :] for k in keys}

    ar = {k: jnp.concatenate([at[k], rt[k]], axis=0) for k in keys}
    xb = {k: mm_nt(ar[k], bt[k]) for k in keys}
    xk = {k: mm_nt(ar[k], kt[k]) for k in keys}
    l_ab = {k: jnp.where(strict, xb[k][:c], 0.0) for k in keys}
    m_rb = {k: jnp.where(incl, xb[k][c:], 0.0) for k in keys}
    l_ak = {k: jnp.where(strict, xk[k][:c], 0.0) for k in keys}
    m_rk = {k: jnp.where(incl, xk[k][c:], 0.0) for k in keys}
    t_inv = {k: eye + l_ab[k] for k in keys}
    l_pow = l_ab
    span = 1
    while 2 * span < c:
        l_pow = {k: mm(l_pow[k], l_pow[k]) for k in keys}
        t_inv = {k: mm(t_inv[k], eye + l_pow[k]) for k in keys}
        span *= 2
    lv = {k: mm(l_ak[k], vv[k]) for k in keys}
    w_m = {k: mm(t_inv[k], at[k]) for k in keys}
    y_m = {k: mm(t_inv[k], lv[k]) for k in keys}
    wtb = {k: mm_tn(w_m[k], bt[k]) for k in keys}
    s_add = {k: mm_tn(y_m[k], bt[k]) + mm_tn(vv[k], kt[k]) for k in keys}
    rw = {k: rt[k] + mm(m_rb[k], w_m[k]) for k in keys}
    o_add = {k: mm(m_rb[k], y_m[k]) + mm(m_rk[k], vv[k]) for k in keys}

    o_heads = {}
    for sq in range(nseq):
        for h in range(nh):
            s = s_scr[sq, h]
            for g in range(n_chunks):
                k = (sq, g, h)
                o_heads[k] = mm_nt(rw[k], s) + o_add[k]
                last = sq * rows_seq + (g + 1) * c - 1
                s = (s + mm(s, wtb[k]) + s_add[k]) * ep_v[h][last:last + 1, :]
            s_scr[sq, h] = s

    o_all = jnp.concatenate(
        [jnp.concatenate([o_heads[(sq, g, h)] for h in range(nh)], axis=1)
         for sq in range(nseq) for g in range(n_chunks)], axis=0)
    mu_o = seg(o_all) * (1.0 / HEAD_DIM)
    dev = o_all - mu_o
    var = seg(dev * dev) * (1.0 / HEAD_DIM)
    y = dev * lax.rsqrt(var + GN_EPS) * gng_ref[...] + gnb_ref[...] + bonus_all
    y_ref[...] = y.reshape(nseq, rows_seq, wc)

    @pl.when(ti == pl.num_programs(1) - 1)
    def _():
        sfin_ref[...] = s_scr[...]


def _rwkv(pc, shift0, wkv0, w, *, c, n_chunks, nseq):
    b, t, ws = pc.shape
    nh = wkv0.shape[1]
    wc = nh * HEAD_DIM
    rows = c * n_chunks
    assert t % rows == 0 and b % nseq == 0 and nh % 2 == 0
    full = lambda a: pl.BlockSpec(a.shape, lambda i, j: (0,) * a.ndim)
    names = ("c_mu", "c_w0", "c_wb", "c_a0", "c_ab", "c_kk", "c_ka", "c_rk", "c_gn_g", "c_gn_b", "ones_c")
    params = [w[n] for n in names]
    state_spec = pl.BlockSpec((nseq, nh, HEAD_DIM, HEAD_DIM), lambda i, j: (i, 0, 0, 0))
    return pl.pallas_call(
        functools.partial(_rwkv_kernel, nh=nh, c=c, lowp=c >= 16),
        grid=(b // nseq, t // rows),
        in_specs=[pl.BlockSpec((nseq, rows, ws), lambda i, j: (i, j, 0)),
                  pl.BlockSpec((nseq, 1, ws), lambda i, j: (i, 0, 0)),
                  state_spec] + [full(p) for p in params],
        out_specs=[pl.BlockSpec((nseq, rows, wc), lambda i, j: (i, j, 0)), state_spec],
        out_shape=[jax.ShapeDtypeStruct((b, t, wc), F32), jax.ShapeDtypeStruct((b, nh, HEAD_DIM, HEAD_DIM), F32)],
        scratch_shapes=[pltpu.VMEM((nseq, nh, HEAD_DIM, HEAD_DIM), F32), pltpu.VMEM((nseq, 1, ws), F32)],
        compiler_params=_cparams(("arbitrary", "arbitrary")),
        name="rwkv7",
    )(pc, shift0, wkv0, *params)


def _out_kernel(ya_ref, yb_ref, yc_ref, z_ref, x_ref, mod_ref, wo_ref, o_ref):
    x = x_ref[...]
    nb, tt, d = x.shape
    m = nb * tt
    z = z_ref[...]
    wa = ya_ref.shape[-1]
    wb = yb_ref.shape[-1]
    ya = (ya_ref[...] * z[:, :, :wa]).reshape(m, wa).astype(BF16)
    yb = (yb_ref[...] * z[:, :, wa:wa + wb]).reshape(m, wb).astype(BF16)
    yc = (yc_ref[...] * z[:, :, wa + wb:]).reshape(m, -1).astype(BF16)
    y = (jnp.dot(ya, wo_ref[:wa, :], preferred_element_type=F32)
         + jnp.dot(yb, wo_ref[wa:wa + wb, :], preferred_element_type=F32)
         + jnp.dot(yc, wo_ref[wa + wb:, :], preferred_element_type=F32))
    o_ref[...] = x + mod_ref[...][:, :, 2 * d:] * y.reshape(nb, tt, d)


def _out_proj(ya, yb, yc, z, x, mod, wo, *, nb_blk, t_blk):
    b, t, d = x.shape
    row = lambda a: pl.BlockSpec((nb_blk, t_blk, a.shape[-1]), lambda i, j: (i, j, 0))
    return pl.pallas_call(
        _out_kernel,
        grid=(b // nb_blk, t // t_blk),
        in_specs=[row(ya), row(yb), row(yc), row(z), row(x),
                  pl.BlockSpec((nb_blk, 1, 3 * d), lambda i, j: (i, 0, 0)),
                  pl.BlockSpec(wo.shape, lambda i, j: (0, 0))],
        out_specs=row(x),
        out_shape=jax.ShapeDtypeStruct((b, t, d), F32),
        compiler_params=_cparams(("arbitrary", "arbitrary")),
        name="out_proj",
    )(ya, yb, yc, z, x, mod, wo)


def _layer_weights(l, w_in, w_out, g_norm, a_ln_g, a_ln_b, a_ws, a_bs, b_qg, b_kg, b_fb, c_mu, c_w0, c_wb,
                   c_a0, c_ab, c_kk, c_ka, c_rk, c_gn_g, c_gn_b):
    wa_w = a_ln_g.shape[-1]
    nhb = b_fb.shape[-1]
    wb_w = nhb * HEAD_DIM
    wc_w = c_w0.shape[-1]
    lora_w, lora_a = c_wb.shape[1], c_ab.shape[1]
    o_q = 2 * wa_w
    o_k = o_q + wb_w
    o_v = o_k + wb_w
    o_f = o_v + wb_w
    o_c = o_f + nhb
    o_z = o_c + 3 * wc_w + lora_w + lora_a
    wi = w_in[l]
    row = lambda a: a.reshape(1, -1)
    head_lane = jnp.arange(wb_w) // HEAD_DIM
    head_lane_c = jnp.arange(wc_w) // HEAD_DIM
    w = dict(
        g=row(g_norm[l]),
        wa=wi[:, :o_q].astype(BF16), wq=wi[:, o_q:o_k].astype(BF16),
        wk=wi[:, o_k:o_v].astype(BF16), wv=wi[:, o_v:o_f].astype(BF16),
        wf=jnp.pad(wi[:, o_f:o_c], ((0, 0), (0, LANES - nhb))).astype(BF16),
        wc=wi[:, o_c:o_z].astype(BF16), wz=wi[:, o_z:].astype(BF16),
        wo=w_out[l].astype(BF16),
        ones_bd=(head_lane[:, None] == head_lane[None, :]).astype(BF16),
        ones_c=(head_lane_c[:, None] == head_lane_c[None, :]).astype(BF16),
        qg_row=row(jnp.tile(b_qg[l], nhb)), kg_row=row(jnp.tile(b_kg[l], nhb)),
        qg_col=b_qg[l].reshape(HEAD_DIM, 1), kg_col=b_kg[l].reshape(HEAD_DIM, 1),
        fb=row(jnp.pad(b_fb[l], (0, LANES - nhb))),
        a_ln_g=row(a_ln_g[l]), a_ln_b=row(a_ln_b[l]), a_ws=a_ws[l],
        a_bs_exp=jnp.repeat(a_bs[l].T, HEAD_DIM, axis=1),
        c_mu=row(c_mu[l]), c_w0=row(c_w0[l]), c_wb=c_wb[l], c_a0=row(c_a0[l]), c_ab=c_ab[l],
        c_kk=row(c_kk[l]), c_ka=row(c_ka[l]), c_rk=row(c_rk[l]), c_gn_g=row(c_gn_g[l]), c_gn_b=row(c_gn_b[l]),
    )
    w["wq_t"] = w["wq"].T
    w["wk_t"] = w["wk"].T
    w["wv_t"] = w["wv"].T
    return w, nhb


def kernel(x_prompt, x_sample, c_prompt, c_sample, cache_fox_k, cache_fox_v, cache_fox_logf, page_table, state_wkv, state_shift, w_ada, b_ada, g_norm, w_in, w_out, a_ln_g, a_ln_b, a_ws, a_bs, b_qg, b_kg, b_fb, c_mu, c_w0, c_wb, c_a0, c_ab, c_kk, c_ka, c_rk, c_gn_g, c_gn_b):
    nl = w_in.shape[0]
    bp, tp, d = x_prompt.shape
    bs, ts, _ = x_sample.shape
    nhc = state_wkv.shape[2]
    n_pool, ps = cache_fox_k.shape[1], cache_fox_k.shape[2]

    n_c = bp + bs
    n_pad = -(-n_c // 8) * 8
    c_all = jnp.pad(jnp.concatenate([c_prompt, c_sample], axis=0), ((0, n_pad - n_c), (0, 0)))
    mod_all = _ada_mod(c_all, w_ada, b_ada)

    kt_pool = jnp.transpose(cache_fox_k, (0, 1, 3, 4, 2)).reshape(nl, n_pool, -1, ps)
    vt_pool = jnp.transpose(cache_fox_v, (0, 1, 3, 4, 2)).reshape(nl, n_pool, -1, ps)
    logf_t = jnp.transpose(cache_fox_logf, (0, 1, 3, 2))

    tq = min(512, tp)
    t_blk = min(256, tp)
    rw_chunk = min(HEAD_DIM, tp)
    rw_chunks_per_step = max(1, min(4, tp // rw_chunk))
    xp, xs = x_prompt, x_sample
    outs = {k: [] for k in ("kp", "vp", "fp", "ks", "vs", "fs", "cvs", "wp", "wsl", "sp", "ssl")}
    for l in range(nl):
        w, nhb = _layer_weights(l, w_in, w_out, g_norm, a_ln_g, a_ln_b, a_ws, a_bs, b_qg, b_kg, b_fb, c_mu,
                                c_w0, c_wb, c_a0, c_ab, c_kk, c_ka, c_rk, c_gn_g, c_gn_b)
        mod_p = mod_all[l, :bp][:, None, :]
        mod_s = mod_all[l, bp:n_c][:, None, :]

        wp_t = dict(w, wq=w["wq_t"], wk=w["wk_t"], wv=w["wv_t"])
        pa, qt, kt, vt, kb16, vt16, f, pc, z = _in_proj(xp, mod_p, wp_t, kv_t=True, nb_blk=1, t_blk=t_blk)
        (ya,) = _chunk_mix(pa, w, with_va=False)
        ct, ft, ca = _cumsum_logf(f, nhb, tq)
        yb = _fox_prompt(qt, kb16, vt16, ca, ct, nhb, tq)
        yc, wkv_p = _rwkv(pc, jnp.zeros((bp, 1, pc.shape[-1]), F32), jnp.zeros((bp, nhc, HEAD_DIM, HEAD_DIM), F32),
                          w, c=rw_chunk, n_chunks=rw_chunks_per_step, nseq=1)
        xp = _out_proj(ya, yb, yc, z, xp, mod_p, w["wo"], nb_blk=1, t_blk=t_blk)
        outs["kp"].append(jnp.transpose(kt.reshape(bp, nhb, HEAD_DIM, tp), (0, 3, 1, 2)))
        outs["vp"].append(jnp.transpose(vt.reshape(bp, nhb, HEAD_DIM, tp), (0, 3, 1, 2)))
        outs["fp"].append(jnp.transpose(ft, (0, 2, 1)))
        outs["wp"].append(wkv_p)
        outs["sp"].append(pc[:, -1, :])

        pa, q, kn, vn, fn, pc, z = _in_proj(xs, mod_s, w, kv_t=False, nb_blk=bs, t_blk=ts)
        ya, va = _chunk_mix(pa, w, with_va=True)
        cs = _cumsum_pages(page_table, logf_t, l)
        yb = _fox_sample(page_table, q, kn, vn, fn, cs, kt_pool, vt_pool, l, nhb)
        yc, wkv_s = _rwkv(pc, state_shift[l][:, None, :], state_wkv[l], w, c=ts, n_chunks=1, nseq=min(8, bs))
        xs = _out_proj(ya, yb, yc, z, xs, mod_s, w["wo"], nb_blk=bs, t_blk=ts)
        outs["ks"].append(kn.reshape(bs, ts, nhb, HEAD_DIM))
        outs["vs"].append(vn.reshape(bs, ts, nhb, HEAD_DIM))
        outs["fs"].append(fn[:, :, :nhb])
        outs["cvs"].append(va.reshape(bs, ts, -1, HEAD_DIM))
        outs["wsl"].append(wkv_s)
        outs["ssl"].append(pc[:, -1, :])

    st = lambda k: jnp.stack(outs[k])
    return (xp, xs, st("kp"), st("vp"), st("fp"), st("ks"), st("vs"), st("fs"), st("cvs"),
            st("wp"), st("wsl"), st("sp"), st("ssl"))
```

```python
import functools
import math

import jax
import jax.numpy as jnp
from jax import lax
from jax.experimental import pallas as pl
from jax.experimental.pallas import tpu as pltpu

F32 = jnp.float32
BF16 = jnp.bfloat16
HEAD_DIM = 64
LANES = 128
NORM_EPS = 1e-6
LN_EPS = 1e-5
GN_EPS = 64e-5
DECAY_MAX = math.exp(-0.5)
LOG2E = math.log2(math.e)
PAGES_PER_STEP = 32
BIAS_LANES = 4
NEG_BIG = -1e30
VMEM_LIMIT_BYTES = 56 * 1024 * 1024
HIGHEST = lax.Precision.HIGHEST


def _sigmoid(x):
    return 1.0 / (1.0 + jnp.exp(-x))


def _silu(x):
    return x * _sigmoid(x)


def _gelu_tanh(x):
    return 0.5 * x * (1.0 + jnp.tanh(math.sqrt(2.0 / math.pi) * (x + 0.044715 * (x * x * x))))


def _log_sigmoid(x):
    return jnp.minimum(x, 0.0) - jnp.log1p(jnp.exp(-jnp.abs(x)))


def _mm(a, b):
    return jnp.dot(a.astype(BF16), b.astype(BF16), preferred_element_type=F32)


def _mm_nt(a, b):
    return lax.dot_general(a.astype(BF16), b.astype(BF16), (((1,), (1,)), ((), ())),
                           preferred_element_type=F32)


def _mm_tn(a, b):
    return lax.dot_general(a.astype(BF16), b.astype(BF16), (((0,), (0,)), ((), ())),
                           preferred_element_type=F32)


def _split3(x):
    hi = x.astype(BF16)
    r1 = x - hi.astype(F32)
    mid = r1.astype(BF16)
    lo = (r1 - mid.astype(F32)).astype(BF16)
    return hi, mid, lo


def _mm_exact_l(m01, x):
    m = m01.astype(BF16)
    hi, mid, lo = _split3(x)
    d = lambda p: jnp.dot(m, p, preferred_element_type=F32)
    return d(hi) + d(mid) + d(lo)


def _mm_exact_r(x, m01):
    m = m01.astype(BF16)
    hi, mid, lo = _split3(x)
    d = lambda p: jnp.dot(p, m, preferred_element_type=F32)
    return d(hi) + d(mid) + d(lo)


def _iota(shape, dim):
    return lax.broadcasted_iota(jnp.int32, shape, dim)


def _cparams(sem):
    return pltpu.CompilerParams(dimension_semantics=sem, vmem_limit_bytes=VMEM_LIMIT_BYTES)


def _ada_kernel(c_ref, w_ref, b_ref, o_ref):
    c = c_ref[...]
    o_ref[...] = jnp.dot(_silu(c), w_ref[...], precision=HIGHEST, preferred_element_type=F32) + b_ref[...]


def _ada_mod(c_all, w_ada, b_ada):
    n, d = c_all.shape
    nl = w_ada.shape[0]
    return pl.pallas_call(
        _ada_kernel,
        grid=(nl, 3),
        in_specs=[pl.BlockSpec((n, d), lambda l, j: (0, 0)),
                  pl.BlockSpec((None, d, d), lambda l, j: (l, 0, j)),
                  pl.BlockSpec((None, 1, d), lambda l, j: (l, 0, j))],
        out_specs=pl.BlockSpec((None, n, d), lambda l, j: (l, 0, j)),
        out_shape=jax.ShapeDtypeStruct((nl, n, 3 * d), F32),
        compiler_params=_cparams(("arbitrary", "arbitrary")),
        name="ada_mod",
    )(c_all, w_ada, b_ada.reshape(nl, 1, 3 * d))


def _head_rms_rows(x, ones_bd, gain):
    xx = x * x
    hi = xx.astype(BF16)
    lo = (xx - hi.astype(F32)).astype(BF16)
    ss = jnp.dot(hi, ones_bd, preferred_element_type=F32) + jnp.dot(lo, ones_bd, preferred_element_type=F32)
    return x * lax.rsqrt(ss * (1.0 / HEAD_DIM) + NORM_EPS) * gain


def _in_kernel(x_ref, mod_ref, g_ref, wa_ref, wq_ref, wk_ref, wv_ref, wc_ref, wz_ref, wf_ref, ones_ref,
               qg_ref, kg_ref, fb_ref, *out_refs, kv_t):
    if kv_t:
        pa_ref, q_ref, k_ref, v_ref, kb_ref, vb_ref, f_ref, pc_ref, z_ref = out_refs
    else:
        pa_ref, q_ref, k_ref, v_ref, f_ref, pc_ref, z_ref = out_refs
    x = x_ref[...]
    nb, tt, d = x.shape
    m = nb * tt
    xn = x * lax.rsqrt(jnp.mean(x * x, axis=-1, keepdims=True) + NORM_EPS)
    mod = mod_ref[...]
    h = xn * g_ref[...] * (1.0 + mod[:, :, d:2 * d]) + mod[:, :, :d]
    hb = h.reshape(m, d).astype(BF16)

    def proj(w_ref):
        return jnp.dot(hb, w_ref[...], preferred_element_type=F32)

    def put(ref, val):
        ref[...] = val.reshape(nb, tt, val.shape[-1])

    put(pa_ref, proj(wa_ref))
    put(pc_ref, proj(wc_ref))
    put(z_ref, _silu(proj(wz_ref)))
    put(f_ref, _log_sigmoid(proj(wf_ref) + fb_ref[...]))
    if kv_t:
        def proj_t(w_ref):
            return lax.dot_general(w_ref[...], hb, (((1,), (1,)), ((), ())), preferred_element_type=F32)

        def head_rms_t(xt, gain_col):
            nh = xt.shape[0] // HEAD_DIM
            x3 = xt.reshape(nh, HEAD_DIM, m)
            ms = jnp.mean(x3 * x3, axis=1, keepdims=True)
            return (x3 * lax.rsqrt(ms + NORM_EPS) * gain_col[None]).reshape(nh * HEAD_DIM, m)

        qt = head_rms_t(proj_t(wq_ref), qg_ref[...]) * (HEAD_DIM ** -0.5 * LOG2E)
        nh = qt.shape[0] // HEAD_DIM
        first_half = _iota((LANES, 1), 0) < HEAD_DIM
        for h in range(nh):
            qp = qt[(h // 2) * LANES:(h // 2 + 1) * LANES, :]
            q_ref[0, h * LANES:(h + 1) * LANES, :] = jnp.where(first_half == (h % 2 == 0), qp, 0.0).astype(BF16)
        kt = head_rms_t(proj_t(wk_ref), kg_ref[...])
        vt = proj_t(wv_ref)
        k_ref[0] = kt
        v_ref[0] = vt
        kb_ref[0] = kt.T.astype(BF16)
        vb_ref[0] = vt.astype(BF16)
    else:
        qn = _head_rms_rows(proj(wq_ref), ones_ref[...], qg_ref[...])
        put(q_ref, qn * (HEAD_DIM ** -0.5))
        put(k_ref, _head_rms_rows(proj(wk_ref), ones_ref[...], kg_ref[...]))
        put(v_ref, proj(wv_ref))


def _in_proj(x, mod, w, *, kv_t, nb_blk, t_blk):
    b, t, d = x.shape
    wb = w["wq"].shape[1]
    grid = (b // nb_blk, t // t_blk)
    row = lambda width: pl.BlockSpec((nb_blk, t_blk, width), lambda i, j: (i, j, 0))
    full = lambda a: pl.BlockSpec(a.shape, lambda i, j: (0,) * a.ndim)
    wa, wq, wk, wv, wc, wz, wf = (w[n] for n in ("wa", "wq", "wk", "wv", "wc", "wz", "wf"))
    rows_out = lambda width, dt=F32: (jax.ShapeDtypeStruct((b, t, width), dt), row(width))
    if kv_t:
        assert nb_blk == 1
        wb = w["wq"].shape[0]
        t_out = lambda rows, dt: (jax.ShapeDtypeStruct((b, rows, t), dt),
                                  pl.BlockSpec((1, rows, t_blk), lambda i, j: (i, 0, j)))
        qg, kg = w["qg_col"], w["kg_col"]
        qkv = [t_out(2 * wb, BF16), t_out(wb, F32), t_out(wb, F32), rows_out(wb, BF16), t_out(wb, BF16)]
    else:
        qg, kg = w["qg_row"], w["kg_row"]
        qkv = [rows_out(wb), rows_out(wb), rows_out(wb)]
    outs = [rows_out(wa.shape[1])] + qkv + [rows_out(LANES), rows_out(wc.shape[1]), rows_out(wz.shape[1])]
    out_shape = [o[0] for o in outs]
    out_specs = [o[1] for o in outs]
    return pl.pallas_call(
        functools.partial(_in_kernel, kv_t=kv_t),
        grid=grid,
        in_specs=[row(d), pl.BlockSpec((nb_blk, 1, 3 * d), lambda i, j: (i, 0, 0)), full(w["g"]),
                  full(wa), full(wq), full(wk), full(wv), full(wc), full(wz), full(wf), full(w["ones_bd"]),
                  full(qg), full(kg), full(w["fb"])],
        out_specs=out_specs,
        out_shape=out_shape,
        compiler_params=_cparams(("arbitrary", "arbitrary")),
        name="in_proj_t" if kv_t else "in_proj",
    )(x, mod, w["g"], wa, wq, wk, wv, wc, wz, wf, w["ones_bd"], qg, kg, w["fb"])


def _mix_kernel(pa_ref, lng_ref, lnb_ref, ws_ref, bs_ref, ya_ref, *va_refs, lowp):
    pa = pa_ref[0]
    tc = pa.shape[0]
    wa = pa.shape[1] // 2
    nh = wa // HEAD_DIM
    u = _gelu_tanh(pa[:, :wa])
    g = _gelu_tanh(pa[:, wa:])
    mu = jnp.mean(g, axis=-1, keepdims=True)
    var = jnp.mean(jnp.square(g - mu), axis=-1, keepdims=True)
    va = (g - mu) * lax.rsqrt(var + LN_EPS) * lng_ref[...] + lnb_ref[...]
    for va_ref in va_refs:
        va_ref[0] = va
    causal = _iota((tc, tc), 1) <= _iota((tc, tc), 0)
    lane_head = _iota((1, wa), 1) // HEAD_DIM
    mix = jnp.zeros((tc, wa), F32)
    for h in range(nh):
        wc = jnp.where(causal, ws_ref[h, :tc, :tc], 0.0)
        vh = jnp.where(lane_head == h, va, 0.0)
        if lowp:
            mix = mix + _mm(wc, vh)
        else:
            mix = mix + jnp.dot(wc, vh, preferred_element_type=F32)
    ya_ref[0] = u * (mix + bs_ref[:tc, :])


def _chunk_mix(pa, w, *, with_va):
    n_out = 2 if with_va else 1
    b, t, w2 = pa.shape
    chunk = w["a_ws"].shape[-1]
    tc = min(chunk, t)
    assert t % tc == 0
    wa = w2 // 2
    full = lambda a: pl.BlockSpec(a.shape, lambda i, j: (0,) * a.ndim)
    return pl.pallas_call(
        functools.partial(_mix_kernel, lowp=tc >= 16),
        grid=(b, t // tc),
        in_specs=[pl.BlockSpec((1, tc, w2), lambda i, j: (i, j, 0)), full(w["a_ln_g"]), full(w["a_ln_b"]),
                  full(w["a_ws"]), full(w["a_bs_exp"])],
        out_specs=[pl.BlockSpec((1, tc, wa), lambda i, j: (i, j, 0))] * n_out,
        out_shape=[jax.ShapeDtypeStruct((b, t, wa), F32)] * n_out,
        compiler_params=_cparams(("arbitrary", "arbitrary")),
        name="chunk_mix",
    )(pa, w["a_ln_g"], w["a_ln_b"], w["a_ws"], w["a_bs_exp"])


def _cum_kernel(f_ref, ct_ref, ft_ref, ca_ref, carry, *, nh):
    @pl.when(pl.program_id(1) == 0)
    def _():
        carry[...] = jnp.zeros_like(carry)

    f = f_ref[0]
    tb = f.shape[0]
    tri = (_iota((tb, tb), 1) <= _iota((tb, tb), 0)).astype(F32)
    c = _mm_exact_l(tri, f) + carry[...]
    carry[...] = c[tb - 1:tb, :]
    c2 = c * LOG2E
    ct_ref[0] = c2.T[:nh, :]
    ft_ref[0] = f.T[:nh, :]
    src = _iota((LANES, LANES), 0)
    dst = _iota((LANES, LANES), 1)
    ca = jnp.zeros((tb, LANES), F32)
    for i, piece in enumerate(_split3(-c2)):
        place = ((dst == BIAS_LANES * src + i) & (src < nh)).astype(BF16)
        ca = ca + jnp.dot(piece, place, preferred_element_type=F32)
    ca_ref[0] = ca.astype(BF16)


def _cumsum_logf(f, nh, tb):
    b, t, _ = f.shape
    row = pl.BlockSpec((1, tb, LANES), lambda i, j: (i, j, 0))
    tr = pl.BlockSpec((1, nh, tb), lambda i, j: (i, 0, j))
    return pl.pallas_call(
        functools.partial(_cum_kernel, nh=nh),
        grid=(b, t // tb),
        in_specs=[row],
        out_specs=[tr, tr, row],
        out_shape=[jax.ShapeDtypeStruct((b, nh, t), F32), jax.ShapeDtypeStruct((b, nh, t), F32),
                   jax.ShapeDtypeStruct((b, t, LANES), BF16)],
        scratch_shapes=[pltpu.VMEM((1, LANES), F32)],
        compiler_params=_cparams(("arbitrary", "arbitrary")),
        name="cumsum_logf",
    )(f)


def _foxp_kernel(qi_ref, ki_ref, qt_ref, k_ref, vt_ref, ca_ref, ct_ref, o_ref, *scratch, nh, tq):
    m_scr, l_scr, acc_scr = scratch[:nh], scratch[nh:2 * nh], scratch[2 * nh:]
    qi = qi_ref[pl.program_id(1)]
    ki = ki_ref[pl.program_id(1)]

    @pl.when(ki == 0)
    def _():
        for h in range(nh):
            m_scr[h][...] = jnp.full_like(m_scr[h], NEG_BIG)
            l_scr[h][...] = jnp.zeros_like(l_scr[h])
            acc_scr[h][...] = jnp.zeros_like(acc_scr[h])

    sel_row = _iota((LANES, tq), 0)
    ones_rows = jnp.ones((16, tq), BF16)

    def block(masked):
        if masked:
            keep = _iota((tq, tq), 0) <= _iota((tq, tq), 1)
        def scores(h):
            pair = slice((h // 2) * LANES, (h // 2 + 1) * LANES)
            pick = ((sel_row >= BIAS_LANES * h) & (sel_row < BIAS_LANES * h + 3)).astype(BF16)
            k_cat = jnp.concatenate([k_ref[0, :, pair], ca_ref[0]], axis=1)
            q_cat = jnp.concatenate([qt_ref[0, h * LANES:(h + 1) * LANES, :], pick], axis=0)
            return jnp.dot(k_cat, q_cat, preferred_element_type=F32)

        s_next = scores(0)
        for h in range(nh):
            s = s_next
            if h + 1 < nh:
                s_next = scores(h + 1)
            if masked:
                s = jnp.where(keep, s, NEG_BIG)
            cq = ct_ref[0, h:h + 1, :]
            m_run = m_scr[h][...]
            m_new = jnp.maximum(m_run, jnp.max(s, axis=0, keepdims=True) + cq)
            alpha = jnp.exp2(m_run - m_new)
            p = jnp.exp2(s - (m_new - cq)).astype(BF16)
            v_ext = jnp.concatenate([vt_ref[0, h * HEAD_DIM:(h + 1) * HEAD_DIM, :], ones_rows], axis=0)
            pv = jnp.dot(v_ext, p, preferred_element_type=F32)
            l_scr[h][...] = alpha * l_scr[h][...] + pv[HEAD_DIM:HEAD_DIM + 1, :]
            m_scr[h][...] = m_new
            acc_scr[h][...] = acc_scr[h][...] * alpha + pv[:HEAD_DIM, :]

    @pl.when(ki < qi)
    def _():
        block(False)

    @pl.when(ki == qi)
    def _():
        block(True)
        for pr in range(nh // 2):
            sl = slice(pr * LANES, (pr + 1) * LANES)
            o_pair = jnp.concatenate([acc_scr[h][...] / l_scr[h][...] for h in (2 * pr, 2 * pr + 1)], axis=0)
            o_ref[0, :, sl] = o_pair.T


def _fox_prompt(qt, kb16, vt16, ca, ct, nh, tq):
    b, _, t = qt.shape
    wb = nh * HEAD_DIM
    nq = t // tq
    pairs = [(q, k) for q in range(nq) for k in range(q + 1)]
    qi_tab = jnp.asarray([p[0] for p in pairs], jnp.int32)
    ki_tab = jnp.asarray([p[1] for p in pairs], jnp.int32)
    q_t = lambda i, s, qi, ki: (i, 0, qi[s])
    kv_t = lambda i, s, qi, ki: (i, 0, ki[s])
    kv_r = lambda i, s, qi, ki: (i, ki[s], 0)
    return pl.pallas_call(
        functools.partial(_foxp_kernel, nh=nh, tq=tq),
        grid_spec=pltpu.PrefetchScalarGridSpec(
            num_scalar_prefetch=2,
            grid=(b, len(pairs)),
            in_specs=[pl.BlockSpec((1, nh * LANES, tq), q_t),
                      pl.BlockSpec((1, tq, wb), kv_r),
                      pl.BlockSpec((1, wb, tq), kv_t),
                      pl.BlockSpec((1, tq, LANES), kv_r),
                      pl.BlockSpec((1, nh, tq), q_t)],
            out_specs=pl.BlockSpec((1, tq, wb), lambda i, s, qi, ki: (i, qi[s], 0)),
            scratch_shapes=([pltpu.VMEM((1, tq), F32)] * (2 * nh) + [pltpu.VMEM((HEAD_DIM, tq), F32)] * nh),
        ),
        out_shape=jax.ShapeDtypeStruct((b, t, wb), F32),
        compiler_params=_cparams(("arbitrary", "arbitrary")),
        name="fox_prompt",
    )(qi_tab, ki_tab, qt, kb16, vt16, ca, ct)


def _cums_kernel(pt_ref, f_hbm, o_ref, buf, sems, *, layer, npg, nh):
    b = pl.program_id(0)
    slot = b % 2

    def page_copy(seq, j, sl):
        return pltpu.make_async_copy(f_hbm.at[layer, pt_ref[seq, j]], buf.at[sl, j], sems.at[sl])

    def start_all(seq, sl):
        def body(j, carry):
            page_copy(seq, j, sl).start()
            return carry
        lax.fori_loop(0, npg, body, 0)

    def wait_all(seq, sl):
        def body(j, carry):
            page_copy(seq, j, sl).wait()
            return carry
        lax.fori_loop(0, npg, body, 0)

    @pl.when(b == 0)
    def _():
        start_all(b, slot)

    @pl.when(b + 1 < pl.num_programs(0))
    def _():
        start_all(b + 1, 1 - slot)

    wait_all(b, slot)
    ps = buf.shape[-1]
    tri_keys = (_iota((ps, ps), 0) <= _iota((ps, ps), 1)).astype(F32)
    x = buf[slot].reshape(npg * nh, ps)
    y = _mm_exact_r(x, tri_keys).reshape(npg, nh, ps)
    run = jnp.zeros((nh, ps), F32)
    for p in range(npg):
        o_ref[0, p] = y[p] + run
        run = run + jnp.broadcast_to(y[p][:, ps - 1:ps], (nh, ps))


def _cumsum_pages(page_table, logf_t, layer):
    bs, npg = page_table.shape
    _, _, nh, ps = logf_t.shape
    return pl.pallas_call(
        functools.partial(_cums_kernel, layer=layer, npg=npg, nh=nh),
        grid_spec=pltpu.PrefetchScalarGridSpec(
            num_scalar_prefetch=1,
            grid=(bs,),
            in_specs=[pl.BlockSpec(memory_space=pl.ANY)],
            out_specs=pl.BlockSpec((1, npg, nh, ps), lambda i, pt: (i, 0, 0, 0)),
            scratch_shapes=[pltpu.VMEM((2, npg, nh, ps), F32), pltpu.SemaphoreType.DMA((2,))],
        ),
        out_shape=jax.ShapeDtypeStruct((bs, npg, nh, ps), F32),
        compiler_params=_cparams(("arbitrary",)),
        name="cumsum_pages",
    )(page_table, logf_t)


def _foxs_kernel(pt_ref, q_ref, kn_ref, vn_ref, fn_ref, cs_ref, cl_ref, *rest, nh, ts, pp):
    kt_refs = rest[:pp]
    vt_refs = rest[pp:2 * pp]
    o_ref = rest[2 * pp]
    m_scr, l_scr, acc_scr, cq_scr = rest[2 * pp + 1:]
    s_idx = pl.program_id(1)
    n_steps = pl.num_programs(1)
    nr = nh * ts
    wb = nh * HEAD_DIM
    ps = cs_ref.shape[-1]
    row_head = _iota((nr, 1), 0) // ts
    row_q = _iota((nr, 1), 0) % ts

    def rows_from_heads(x):
        return jnp.concatenate([jnp.broadcast_to(x[h:h + 1, :], (ts, x.shape[1])) for h in range(nh)], axis=0)

    q = q_ref[0]
    q_rep = jnp.concatenate([q] * nh, axis=0)
    qbd = jnp.where(_iota((nr, wb), 1) // HEAD_DIM == row_head, q_rep, 0.0).astype(BF16)

    fn = fn_ref[0]
    f_pad = jnp.concatenate([fn, jnp.zeros((LANES - ts, LANES), F32)], axis=0)
    a_new = rows_from_heads(f_pad.T[:nh, :])
    tri = (_iota((LANES, LANES), 0) <= _iota((LANES, LANES), 1)).astype(F32)
    c_new = _mm_exact_r(a_new, tri)

    @pl.when(s_idx == 0)
    def _():
        m_scr[...] = jnp.full_like(m_scr, NEG_BIG)
        l_scr[...] = jnp.zeros_like(l_scr)
        acc_scr[...] = jnp.zeros_like(acc_scr)
        total = rows_from_heads(cl_ref[0, 0][:, ps - 1:ps])
        own = jnp.sum(jnp.where(_iota((nr, LANES), 1) == row_q, c_new, 0.0), axis=-1, keepdims=True)
        cq_scr[...] = total + own

    cq = cq_scr[...]

    def online(s_list, v_list, v_is_t):
        m_prev = m_scr[...]
        m_new = m_prev
        for s in s_list:
            m_new = jnp.maximum(m_new, jnp.max(s, axis=-1, keepdims=True))
        alpha = jnp.exp(m_prev - m_new)
        l_add = jnp.zeros_like(m_prev)
        pv = jnp.zeros((nr, wb), F32)
        for s, v in zip(s_list, v_list):
            p = jnp.exp(s - m_new)
            l_add = l_add + jnp.sum(p, axis=-1, keepdims=True)
            if v_is_t:
                pv = pv + lax.dot_general(p.astype(BF16), v, (((1,), (1,)), ((), ())),
                                          preferred_element_type=F32)
            else:
                pv = pv + jnp.dot(p.astype(BF16), v, preferred_element_type=F32)
        m_scr[...] = m_new
        l_scr[...] = alpha * l_scr[...] + l_add
        acc_scr[...] = alpha * acc_scr[...] + pv

    s_list, v_list = [], []
    for j in range(pp):
        ck = rows_from_heads(cs_ref[0, j])
        s = jnp.dot(qbd, kt_refs[j][...].astype(BF16), preferred_element_type=F32) + cq - ck
        s_list.append(s)
        v_list.append(vt_refs[j][...].astype(BF16))
    online(s_list, v_list, True)

    @pl.when(s_idx == n_steps - 1)
    def _():
        kn = jnp.concatenate([kn_ref[0], jnp.zeros((LANES - ts, wb), F32)], axis=0).astype(BF16)
        vn = jnp.concatenate([vn_ref[0], jnp.zeros((LANES - ts, wb), F32)], axis=0).astype(BF16)
        total = rows_from_heads(cl_ref[0, 0][:, ps - 1:ps])
        s = lax.dot_general(qbd, kn, (((1,), (1,)), ((), ())), preferred_element_type=F32) + cq - (total + c_new)
        s = jnp.where(_iota((nr, LANES), 1) <= row_q, s, NEG_BIG)
        online([s], [vn], False)
        o_full = acc_scr[...] / l_scr[...]
        lane_head = _iota((1, wb), 1) // HEAD_DIM
        out = jnp.zeros((ts, wb), F32)
        for h in range(nh):
            out = out + jnp.where(lane_head == h, o_full[h * ts:(h + 1) * ts, :], 0.0)
        o_ref[0] = out


def _fox_sample(page_table, q, kn, vn, fn, cs, kt_pool, vt_pool, layer, nh):
    bs, ts, wb = q.shape
    npg = page_table.shape[1]
    ps = kt_pool.shape[-1]
    pp = min(PAGES_PER_STEP, npg)
    assert npg % pp == 0
    nr = nh * ts

    def page_spec(j):
        return pl.BlockSpec((None, None, wb, ps), lambda b, s, pt: (layer, pt[b, s * pp + j], 0, 0))

    row = lambda width: pl.BlockSpec((1, ts, width), lambda b, s, pt: (b, 0, 0))
    in_specs = [row(wb), row(wb), row(wb), row(LANES),
                pl.BlockSpec((1, pp, nh, ps), lambda b, s, pt: (b, s, 0, 0)),
                pl.BlockSpec((1, 1, nh, ps), lambda b, s, pt: (b, npg - 1, 0, 0))]
    in_specs += [page_spec(j) for j in range(pp)] * 2
    return pl.pallas_call(
        functools.partial(_foxs_kernel, nh=nh, ts=ts, pp=pp),
        grid_spec=pltpu.PrefetchScalarGridSpec(
            num_scalar_prefetch=1,
            grid=(bs, npg // pp),
            in_specs=in_specs,
            out_specs=row(wb),
            scratch_shapes=[pltpu.VMEM((nr, 1), F32), pltpu.VMEM((nr, 1), F32), pltpu.VMEM((nr, wb), F32),
                            pltpu.VMEM((nr, 1), F32)],
        ),
        out_shape=jax.ShapeDtypeStruct((bs, ts, wb), F32),
        compiler_params=_cparams(("arbitrary", "arbitrary")),
        name="fox_sample",
    )(page_table, q, kn, vn, fn, cs, cs, *([kt_pool] * pp), *([vt_pool] * pp))


def _seg_sum(x, ones_bd, lowp):
    if lowp:
        hi = x.astype(BF16)
        lo = (x - hi.astype(F32)).astype(BF16)
        return jnp.dot(hi, ones_bd, preferred_element_type=F32) + jnp.dot(lo, ones_bd, preferred_element_type=F32)
    return jnp.dot(x, ones_bd.astype(F32), precision=HIGHEST, preferred_element_type=F32)


def _rwkv_kernel(pc_ref, sh0_ref, s0_ref, mu_ref, w0_ref, wb_ref, a0_ref, ab_ref, kkw_ref, ka_ref, rk_ref,
                 gng_ref, gnb_ref, ones_ref, y_ref, sfin_ref, s_scr, prev_scr, *, nh, c, lowp):
    ti = pl.program_id(1)
    nseq, rows_seq, ws = pc_ref.shape
    n_chunks = rows_seq // c
    n = nseq * rows_seq
    wc = nh * HEAD_DIM
    lora = wb_ref.shape[0]

    @pl.when(ti == 0)
    def _():
        for sq in range(nseq):
            for p in range(nh // 2):
                s_scr[sq, p] = jnp.concatenate([s0_ref[sq, 2 * p], s0_ref[sq, 2 * p + 1]], axis=1)
        prev_scr[...] = sh0_ref[...]

    pc = pc_ref[...].reshape(n, ws)
    row = _iota((n, 1), 0)
    prev = pltpu.roll(pc, 1, axis=0)
    for sq in range(nseq):
        prev = jnp.where(row == sq * rows_seq, prev_scr[sq], prev)
        prev_scr[sq] = pc[(sq + 1) * rows_seq - 1:(sq + 1) * rows_seq, :]
    m = pc + (prev - pc) * mu_ref[...]
    r_all = m[:, :wc]
    k_raw = m[:, wc:2 * wc]
    v_all = m[:, 2 * wc:3 * wc]
    w_lo = m[:, 3 * wc:3 * wc + lora]
    a_lo = m[:, 3 * wc + lora:]
    hdot = lambda a, b: jnp.dot(a, b, precision=HIGHEST, preferred_element_type=F32)
    lw_all = -DECAY_MAX * _sigmoid(w0_ref[...] + hdot(jnp.tanh(w_lo), wb_ref[...]))
    a_all = _sigmoid(a0_ref[...] + hdot(a_lo, ab_ref[...]))
    kkr_all = k_raw * kkw_ref[...]
    k_all = k_raw * (1.0 + (a_all - 1.0) * ka_ref[...])
    seg = functools.partial(_seg_sum, ones_bd=ones_ref[...], lowp=lowp)
    kk_all = kkr_all * lax.rsqrt(jnp.maximum(seg(kkr_all * kkr_all), 1e-24))
    bonus_all = seg(r_all * k_all * rk_ref[...]) * v_all

    rr = _iota((n, n), 0)
    cc = _iota((n, n), 1)
    tri_chunks = ((cc <= rr) & (cc // c == rr // c)).astype(F32)
    if lowp:
        mm, mm_nt, mm_tn = _mm, _mm_nt, _mm_tn
        cum = _mm_exact_l(tri_chunks, lw_all)
    else:
        mm = lambda a, b: jnp.dot(a, b, preferred_element_type=F32)
        mm_nt = lambda a, b: lax.dot_general(a, b, (((1,), (1,)), ((), ())), preferred_element_type=F32)
        mm_tn = lambda a, b: lax.dot_general(a, b, (((0,), (0,)), ((), ())), preferred_element_type=F32)
        cum = hdot(tri_chunks, lw_all)
    e_pos = jnp.exp(cum)
    e_neg = jnp.exp(-cum)
    at_all = -kk_all * jnp.exp(cum - lw_all)
    bt_all = kk_all * a_all * e_neg
    kt_all = k_all * e_neg
    rt_all = r_all * e_pos

    def blockdiag(x, split):
        lane = _iota((1, x.shape[1]), 1)
        return jnp.concatenate([jnp.where(lane < split, x, 0.0), jnp.where(lane < split, 0.0, x)], axis=0)

    bd = lambda x: blockdiag(x, HEAD_DIM)
    bdc = lambda x: blockdiag(x, c)
    r2 = _iota((c, 2 * c), 0)
    c2 = _iota((c, 2 * c), 1) % c
    incl = c2 <= r2
    strict = c2 < r2
    eye = (c2 == r2).astype(F32)
    first = _iota((1, LANES), 1) < HEAD_DIM
    same_half = (_iota((LANES, LANES), 0) < HEAD_DIM) == (_iota((LANES, LANES), 1) < HEAD_DIM)

    keys = [(sq, g, p) for sq in range(nseq) for g in range(n_chunks) for p in range(nh // 2)]

    def pair_rows(x, k):
        sq, g, p = k
        return x[sq * rows_seq + g * c:sq * rows_seq + (g + 1) * c, p * LANES:(p + 1) * LANES]

    at = {k: pair_rows(at_all, k) for k in keys}
    bt = {k: pair_rows(bt_all, k) for k in keys}
    kt = {k: pair_rows(kt_all, k) for k in keys}
    rt = {k: pair_rows(rt_all, k) for k in keys}
    vv = {k: pair_rows(v_all, k) for k in keys}

    ar = {k: jnp.concatenate([at[k], rt[k]], axis=0) for k in keys}
    xb = {k: mm_nt(ar[k], bd(bt[k])) for k in keys}
    xk = {k: mm_nt(ar[k], bd(kt[k])) for k in keys}
    l_ab = {k: jnp.where(strict, xb[k][:c], 0.0) for k in keys}
    m_rb = {k: jnp.where(incl, xb[k][c:], 0.0) for k in keys}
    l_ak = {k: jnp.where(strict, xk[k][:c], 0.0) for k in keys}
    m_rk = {k: jnp.where(incl, xk[k][c:], 0.0) for k in keys}
    t_inv = {k: eye + l_ab[k] for k in keys}
    l_pow = l_ab
    span = 1
    while 2 * span < c:
        l_pow = {k: mm(l_pow[k], bdc(l_pow[k])) for k in keys}
        t_inv = {k: mm(t_inv[k], bdc(eye + l_pow[k])) for k in keys}
        span *= 2
    lv = {k: mm(l_ak[k], bd(vv[k])) for k in keys}
    w_m = {k: mm(t_inv[k], bd(at[k])) for k in keys}
    y_m = {k: mm(t_inv[k], bd(lv[k])) for k in keys}
    wtb = {k: jnp.where(same_half, mm_tn(w_m[k], bt[k]), 0.0) for k in keys}
    s_sq = {k: mm_tn(y_m[k], bt[k]) + mm_tn(vv[k], kt[k]) for k in keys}
    s_add = {k: jnp.where(first, s_sq[k][:HEAD_DIM], s_sq[k][HEAD_DIM:]) for k in keys}
    rw = {k: rt[k] + mm(m_rb[k], bd(w_m[k])) for k in keys}
    o_add = {k: mm(m_rb[k], bd(y_m[k])) + mm(m_rk[k], bd(vv[k])) for k in keys}

    o_pairs = {}
    for sq in range(nseq):
        for p in range(nh // 2):
            s = s_scr[sq, p]
            for g in range(n_chunks):
                k = (sq, g, p)
                o_pairs[k] = mm_nt(rw[k], bd(s)) + o_add[k]
                last = sq * rows_seq + (g + 1) * c - 1
                s = (s + mm(s, wtb[k]) + s_add[k]) * e_pos[last:last + 1, p * LANES:(p + 1) * LANES]
            s_scr[sq, p] = s

    o_all = jnp.concatenate(
        [jnp.concatenate([o_pairs[(sq, g, p)] for p in range(nh // 2)], axis=1)
         for sq in range(nseq) for g in range(n_chunks)], axis=0)
    mu_o = seg(o_all) * (1.0 / HEAD_DIM)
    dev = o_all - mu_o
    var = seg(dev * dev) * (1.0 / HEAD_DIM)
    y = dev * lax.rsqrt(var + GN_EPS) * gng_ref[...] + gnb_ref[...] + bonus_all
    y_ref[...] = y.reshape(nseq, rows_seq, wc)

    @pl.when(ti == pl.num_programs(1) - 1)
    def _():
        for sq in range(nseq):
            for p in range(nh // 2):
                s = s_scr[sq, p]
                sfin_ref[sq, 2 * p] = s[:, :HEAD_DIM]
                sfin_ref[sq, 2 * p + 1] = s[:, HEAD_DIM:]


def _rwkv(pc, shift0, wkv0, w, *, c, n_chunks, nseq):
    b, t, ws = pc.shape
    nh = wkv0.shape[1]
    wc = nh * HEAD_DIM
    rows = c * n_chunks
    assert t % rows == 0 and b % nseq == 0 and nh % 2 == 0
    full = lambda a: pl.BlockSpec(a.shape, lambda i, j: (0,) * a.ndim)
    names = ("c_mu", "c_w0", "c_wb", "c_a0", "c_ab", "c_kk", "c_ka", "c_rk", "c_gn_g", "c_gn_b", "ones_c")
    params = [w[n] for n in names]
    state_spec = pl.BlockSpec((nseq, nh, HEAD_DIM, HEAD_DIM), lambda i, j: (i, 0, 0, 0))
    return pl.pallas_call(
        functools.partial(_rwkv_kernel, nh=nh, c=c, lowp=c >= 16),
        grid=(b // nseq, t // rows),
        in_specs=[pl.BlockSpec((nseq, rows, ws), lambda i, j: (i, j, 0)),
                  pl.BlockSpec((nseq, 1, ws), lambda i, j: (i, 0, 0)),
                  state_spec] + [full(p) for p in params],
        out_specs=[pl.BlockSpec((nseq, rows, wc), lambda i, j: (i, j, 0)), state_spec],
        out_shape=[jax.ShapeDtypeStruct((b, t, wc), F32), jax.ShapeDtypeStruct((b, nh, HEAD_DIM, HEAD_DIM), F32)],
        scratch_shapes=[pltpu.VMEM((nseq, nh // 2, HEAD_DIM, LANES), F32), pltpu.VMEM((nseq, 1, ws), F32)],
        compiler_params=_cparams(("arbitrary", "arbitrary")),
        name="rwkv7",
    )(pc, shift0, wkv0, *params)


def _out_kernel(ya_ref, yb_ref, yc_ref, z_ref, x_ref, mod_ref, wo_ref, o_ref):
    x = x_ref[...]
    nb, tt, d = x.shape
    m = nb * tt
    z = z_ref[...]
    wa = ya_ref.shape[-1]
    wb = yb_ref.shape[-1]
    ya = (ya_ref[...] * z[:, :, :wa]).reshape(m, wa).astype(BF16)
    yb = (yb_ref[...] * z[:, :, wa:wa + wb]).reshape(m, wb).astype(BF16)
    yc = (yc_ref[...] * z[:, :, wa + wb:]).reshape(m, -1).astype(BF16)
    y = (jnp.dot(ya, wo_ref[:wa, :], preferred_element_type=F32)
         + jnp.dot(yb, wo_ref[wa:wa + wb, :], preferred_element_type=F32)
         + jnp.dot(yc, wo_ref[wa + wb:, :], preferred_element_type=F32))
    o_ref[...] = x + mod_ref[...][:, :, 2 * d:] * y.reshape(nb, tt, d)


def _out_proj(ya, yb, yc, z, x, mod, wo, *, nb_blk, t_blk):
    b, t, d = x.shape
    row = lambda a: pl.BlockSpec((nb_blk, t_blk, a.shape[-1]), lambda i, j: (i, j, 0))
    return pl.pallas_call(
        _out_kernel,
        grid=(b // nb_blk, t // t_blk),
        in_specs=[row(ya), row(yb), row(yc), row(z), row(x),
                  pl.BlockSpec((nb_blk, 1, 3 * d), lambda i, j: (i, 0, 0)),
                  pl.BlockSpec(wo.shape, lambda i, j: (0, 0))],
        out_specs=row(x),
        out_shape=jax.ShapeDtypeStruct((b, t, d), F32),
        compiler_params=_cparams(("arbitrary", "arbitrary")),
        name="out_proj",
    )(ya, yb, yc, z, x, mod, wo)


def _layer_weights(l, w_in, w_out, g_norm, a_ln_g, a_ln_b, a_ws, a_bs, b_qg, b_kg, b_fb, c_mu, c_w0, c_wb,
                   c_a0, c_ab, c_kk, c_ka, c_rk, c_gn_g, c_gn_b):
    wa_w = a_ln_g.shape[-1]
    nhb = b_fb.shape[-1]
    wb_w = nhb * HEAD_DIM
    wc_w = c_w0.shape[-1]
    lora_w, lora_a = c_wb.shape[1], c_ab.shape[1]
    o_q = 2 * wa_w
    o_k = o_q + wb_w
    o_v = o_k + wb_w
    o_f = o_v + wb_w
    o_c = o_f + nhb
    o_z = o_c + 3 * wc_w + lora_w + lora_a
    wi = w_in[l]
    row = lambda a: a.reshape(1, -1)
    head_lane = jnp.arange(wb_w) // HEAD_DIM
    head_lane_c = jnp.arange(wc_w) // HEAD_DIM
    w = dict(
        g=row(g_norm[l]),
        wa=wi[:, :o_q].astype(BF16), wq=wi[:, o_q:o_k].astype(BF16),
        wk=wi[:, o_k:o_v].astype(BF16), wv=wi[:, o_v:o_f].astype(BF16),
        wf=jnp.pad(wi[:, o_f:o_c], ((0, 0), (0, LANES - nhb))).astype(BF16),
        wc=wi[:, o_c:o_z].astype(BF16), wz=wi[:, o_z:].astype(BF16),
        wo=w_out[l].astype(BF16),
        ones_bd=(head_lane[:, None] == head_lane[None, :]).astype(BF16),
        ones_c=(head_lane_c[:, None] == head_lane_c[None, :]).astype(BF16),
        qg_row=row(jnp.tile(b_qg[l], nhb)), kg_row=row(jnp.tile(b_kg[l], nhb)),
        qg_col=b_qg[l].reshape(HEAD_DIM, 1), kg_col=b_kg[l].reshape(HEAD_DIM, 1),
        fb=row(jnp.pad(b_fb[l], (0, LANES - nhb))),
        a_ln_g=row(a_ln_g[l]), a_ln_b=row(a_ln_b[l]), a_ws=a_ws[l],
        a_bs_exp=jnp.repeat(a_bs[l].T, HEAD_DIM, axis=1),
        c_mu=row(c_mu[l]), c_w0=row(c_w0[l]), c_wb=c_wb[l], c_a0=row(c_a0[l]), c_ab=c_ab[l],
        c_kk=row(c_kk[l]), c_ka=row(c_ka[l]), c_rk=row(c_rk[l]), c_gn_g=row(c_gn_g[l]), c_gn_b=row(c_gn_b[l]),
    )
    w["wq_t"] = w["wq"].T
    w["wk_t"] = w["wk"].T
    w["wv_t"] = w["wv"].T
    return w, nhb


def kernel(x_prompt, x_sample, c_prompt, c_sample, cache_fox_k, cache_fox_v, cache_fox_logf, page_table, state_wkv, state_shift, w_ada, b_ada, g_norm, w_in, w_out, a_ln_g, a_ln_b, a_ws, a_bs, b_qg, b_kg, b_fb, c_mu, c_w0, c_wb, c_a0, c_ab, c_kk, c_ka, c_rk, c_gn_g, c_gn_b):
    nl = w_in.shape[0]
    bp, tp, d = x_prompt.shape
    bs, ts, _ = x_sample.shape
    nhc = state_wkv.shape[2]
    n_pool, ps = cache_fox_k.shape[1], cache_fox_k.shape[2]

    n_c = bp + bs
    n_pad = -(-n_c // 8) * 8
    c_all = jnp.pad(jnp.concatenate([c_prompt, c_sample], axis=0), ((0, n_pad - n_c), (0, 0)))
    mod_all = _ada_mod(c_all, w_ada, b_ada)

    kt_pool = jnp.transpose(cache_fox_k, (0, 1, 3, 4, 2)).reshape(nl, n_pool, -1, ps)
    vt_pool = jnp.transpose(cache_fox_v, (0, 1, 3, 4, 2)).reshape(nl, n_pool, -1, ps)
    logf_t = jnp.transpose(cache_fox_logf, (0, 1, 3, 2))

    tq = min(512, tp)
    t_blk = min(512, tp)
    rw_chunk = min(HEAD_DIM, tp)
    rw_chunks_per_step = max(1, min(8, tp // rw_chunk))
    xp, xs = x_prompt, x_sample
    outs = {k: [] for k in ("kp", "vp", "fp", "ks", "vs", "fs", "cvs", "wp", "wsl", "sp", "ssl")}
    for l in range(nl):
        w, nhb = _layer_weights(l, w_in, w_out, g_norm, a_ln_g, a_ln_b, a_ws, a_bs, b_qg, b_kg, b_fb, c_mu,
                                c_w0, c_wb, c_a0, c_ab, c_kk, c_ka, c_rk, c_gn_g, c_gn_b)
        mod_p = mod_all[l, :bp][:, None, :]
        mod_s = mod_all[l, bp:n_c][:, None, :]

        wp_t = dict(w, wq=w["wq_t"], wk=w["wk_t"], wv=w["wv_t"])
        pa, qt, kt, vt, kb16, vt16, f, pc, z = _in_proj(xp, mod_p, wp_t, kv_t=True, nb_blk=1, t_blk=t_blk)
        (ya,) = _chunk_mix(pa, w, with_va=False)
        ct, ft, ca = _cumsum_logf(f, nhb, tq)
        yb = _fox_prompt(qt, kb16, vt16, ca, ct, nhb, tq)
        yc, wkv_p = _rwkv(pc, jnp.zeros((bp, 1, pc.shape[-1]), F32), jnp.zeros((bp, nhc, HEAD_DIM, HEAD_DIM), F32),
                          w, c=rw_chunk, n_chunks=rw_chunks_per_step, nseq=1)
        xp = _out_proj(ya, yb, yc, z, xp, mod_p, w["wo"], nb_blk=1, t_blk=t_blk)
        outs["kp"].append(jnp.transpose(kt.reshape(bp, nhb, HEAD_DIM, tp), (0, 3, 1, 2)))
        outs["vp"].append(jnp.transpose(vt.reshape(bp, nhb, HEAD_DIM, tp), (0, 3, 1, 2)))
        outs["fp"].append(jnp.transpose(ft, (0, 2, 1)))
        outs["wp"].append(wkv_p)
        outs["sp"].append(pc[:, -1, :])

        pa, q, kn, vn, fn, pc, z = _in_proj(xs, mod_s, w, kv_t=False, nb_blk=bs, t_blk=ts)
        ya, va = _chunk_mix(pa, w, with_va=True)
        cs = _cumsum_pages(page_table, logf_t, l)
        yb = _fox_sample(page_table, q, kn, vn, fn, cs, kt_pool, vt_pool, l, nhb)
        yc, wkv_s = _rwkv(pc, state_shift[l][:, None, :], state_wkv[l], w, c=ts, n_chunks=1, nseq=min(8, bs))
        xs = _out_proj(ya, yb, yc, z, xs, mod_s, w["wo"], nb_blk=bs, t_blk=ts)
        outs["ks"].append(kn.reshape(bs, ts, nhb, HEAD_DIM))
        outs["vs"].append(vn.reshape(bs, ts, nhb, HEAD_DIM))
        outs["fs"].append(fn[:, :, :nhb])
        outs["cvs"].append(va.reshape(bs, ts, -1, HEAD_DIM))
        outs["wsl"].append(wkv_s)
        outs["ssl"].append(pc[:, -1, :])

    st = lambda k: jnp.stack(outs[k])
    return (xp, xs, st("kp"), st("vp"), st("fp"), st("ks"), st("vs"), st("fs"), st("cvs"),
            st("wp"), st("wsl"), st("sp"), st("ssl"))
```

```python
import functools
import math

import jax
import jax.numpy as jnp
from jax import lax
from jax.experimental import pallas as pl
from jax.experimental.pallas import tpu as pltpu

F32 = jnp.float32
BF16 = jnp.bfloat16
HEAD_DIM = 64
LANES = 128
NORM_EPS = 1e-6
LN_EPS = 1e-5
GN_EPS = 64e-5
DECAY_MAX = math.exp(-0.5)
LOG2E = math.log2(math.e)
PAGES_PER_STEP = 32
BIAS_LANES = 4
NEG_BIG = -1e30
VMEM_LIMIT_BYTES = 56 * 1024 * 1024
HIGHEST = lax.Precision.HIGHEST


def _sigmoid(x):
    return 1.0 / (1.0 + jnp.exp(-x))


def _silu(x):
    return x * _sigmoid(x)


def _gelu_tanh(x):
    return 0.5 * x * (1.0 + jnp.tanh(math.sqrt(2.0 / math.pi) * (x + 0.044715 * (x * x * x))))


def _log_sigmoid(x):
    return jnp.minimum(x, 0.0) - jnp.log1p(jnp.exp(-jnp.abs(x)))


def _mm(a, b):
    return jnp.dot(a.astype(BF16), b.astype(BF16), preferred_element_type=F32)


def _mm_nt(a, b):
    return lax.dot_general(a.astype(BF16), b.astype(BF16), (((1,), (1,)), ((), ())),
                           preferred_element_type=F32)


def _mm_tn(a, b):
    return lax.dot_general(a.astype(BF16), b.astype(BF16), (((0,), (0,)), ((), ())),
                           preferred_element_type=F32)


def _split3(x):
    hi = x.astype(BF16)
    r1 = x - hi.astype(F32)
    mid = r1.astype(BF16)
    lo = (r1 - mid.astype(F32)).astype(BF16)
    return hi, mid, lo


def _mm_exact_l(m01, x):
    m = m01.astype(BF16)
    hi, mid, lo = _split3(x)
    d = lambda p: jnp.dot(m, p, preferred_element_type=F32)
    return d(hi) + d(mid) + d(lo)


def _mm_exact_r(x, m01):
    m = m01.astype(BF16)
    hi, mid, lo = _split3(x)
    d = lambda p: jnp.dot(p, m, preferred_element_type=F32)
    return d(hi) + d(mid) + d(lo)


def _iota(shape, dim):
    return lax.broadcasted_iota(jnp.int32, shape, dim)


def _cparams(sem):
    return pltpu.CompilerParams(dimension_semantics=sem, vmem_limit_bytes=VMEM_LIMIT_BYTES)


def _ada_kernel(c_ref, w_ref, b_ref, o_ref):
    c = c_ref[...]
    o_ref[...] = jnp.dot(_silu(c), w_ref[...], precision=HIGHEST, preferred_element_type=F32) + b_ref[...]


def _ada_mod(c_all, w_ada, b_ada):
    n, d = c_all.shape
    nl = w_ada.shape[0]
    return pl.pallas_call(
        _ada_kernel,
        grid=(nl, 3),
        in_specs=[pl.BlockSpec((n, d), lambda l, j: (0, 0)),
                  pl.BlockSpec((None, d, d), lambda l, j: (l, 0, j)),
                  pl.BlockSpec((None, 1, d), lambda l, j: (l, 0, j))],
        out_specs=pl.BlockSpec((None, n, d), lambda l, j: (l, 0, j)),
        out_shape=jax.ShapeDtypeStruct((nl, n, 3 * d), F32),
        compiler_params=_cparams(("arbitrary", "arbitrary")),
        name="ada_mod",
    )(c_all, w_ada, b_ada.reshape(nl, 1, 3 * d))


def _head_rms_rows(x, ones_bd, gain):
    xx = x * x
    hi = xx.astype(BF16)
    lo = (xx - hi.astype(F32)).astype(BF16)
    ss = jnp.dot(hi, ones_bd, preferred_element_type=F32) + jnp.dot(lo, ones_bd, preferred_element_type=F32)
    return x * lax.rsqrt(ss * (1.0 / HEAD_DIM) + NORM_EPS) * gain


def _in_kernel(x_ref, mod_ref, g_ref, wa_ref, wq_ref, wk_ref, wv_ref, wc_ref, wz_ref, wf_ref, ones_ref,
               qg_ref, kg_ref, fb_ref, *out_refs, kv_t):
    if kv_t:
        pa_ref, q_ref, k_ref, v_ref, kb_ref, vb_ref, f_ref, pc_ref, z_ref = out_refs
    else:
        pa_ref, q_ref, k_ref, v_ref, f_ref, pc_ref, z_ref = out_refs
    x = x_ref[...]
    nb, tt, d = x.shape
    m = nb * tt
    xn = x * lax.rsqrt(jnp.mean(x * x, axis=-1, keepdims=True) + NORM_EPS)
    mod = mod_ref[...]
    h = xn * g_ref[...] * (1.0 + mod[:, :, d:2 * d]) + mod[:, :, :d]
    hb = h.reshape(m, d).astype(BF16)

    def proj(w_ref):
        return jnp.dot(hb, w_ref[...], preferred_element_type=F32)

    def put(ref, val):
        ref[...] = val.reshape(nb, tt, val.shape[-1])

    put(pa_ref, proj(wa_ref))
    put(pc_ref, proj(wc_ref))
    put(z_ref, _silu(proj(wz_ref)))
    put(f_ref, _log_sigmoid(proj(wf_ref) + fb_ref[...]))
    if kv_t:
        def proj_t(w_ref):
            return lax.dot_general(w_ref[...], hb, (((1,), (1,)), ((), ())), preferred_element_type=F32)

        def head_rms_t(xt, gain_col):
            nh = xt.shape[0] // HEAD_DIM
            x3 = xt.reshape(nh, HEAD_DIM, m)
            ms = jnp.mean(x3 * x3, axis=1, keepdims=True)
            return (x3 * lax.rsqrt(ms + NORM_EPS) * gain_col[None]).reshape(nh * HEAD_DIM, m)

        qt = head_rms_t(proj_t(wq_ref), qg_ref[...]) * (HEAD_DIM ** -0.5 * LOG2E)
        nh = qt.shape[0] // HEAD_DIM
        first_half = _iota((LANES, 1), 0) < HEAD_DIM
        for h in range(nh):
            qp = qt[(h // 2) * LANES:(h // 2 + 1) * LANES, :]
            q_ref[0, h * LANES:(h + 1) * LANES, :] = jnp.where(first_half == (h % 2 == 0), qp, 0.0).astype(BF16)
        kt = head_rms_t(proj_t(wk_ref), kg_ref[...])
        vt = proj_t(wv_ref)
        k_ref[0] = kt
        v_ref[0] = vt
        kb_ref[0] = kt.T.astype(BF16)
        vb_ref[0] = vt.astype(BF16)
    else:
        qn = _head_rms_rows(proj(wq_ref), ones_ref[...], qg_ref[...])
        put(q_ref, qn * (HEAD_DIM ** -0.5))
        put(k_ref, _head_rms_rows(proj(wk_ref), ones_ref[...], kg_ref[...]))
        put(v_ref, proj(wv_ref))


def _in_proj(x, mod, w, *, kv_t, nb_blk, t_blk):
    b, t, d = x.shape
    wb = w["wq"].shape[1]
    grid = (b // nb_blk, t // t_blk)
    row = lambda width: pl.BlockSpec((nb_blk, t_blk, width), lambda i, j: (i, j, 0))
    full = lambda a: pl.BlockSpec(a.shape, lambda i, j: (0,) * a.ndim)
    wa, wq, wk, wv, wc, wz, wf = (w[n] for n in ("wa", "wq", "wk", "wv", "wc", "wz", "wf"))
    rows_out = lambda width, dt=F32: (jax.ShapeDtypeStruct((b, t, width), dt), row(width))
    if kv_t:
        assert nb_blk == 1
        wb = w["wq"].shape[0]
        t_out = lambda rows, dt: (jax.ShapeDtypeStruct((b, rows, t), dt),
                                  pl.BlockSpec((1, rows, t_blk), lambda i, j: (i, 0, j)))
        qg, kg = w["qg_col"], w["kg_col"]
        qkv = [t_out(2 * wb, BF16), t_out(wb, F32), t_out(wb, F32), rows_out(wb, BF16), t_out(wb, BF16)]
    else:
        qg, kg = w["qg_row"], w["kg_row"]
        qkv = [rows_out(wb), rows_out(wb), rows_out(wb)]
    outs = [rows_out(wa.shape[1])] + qkv + [rows_out(LANES), rows_out(wc.shape[1]), rows_out(wz.shape[1])]
    out_shape = [o[0] for o in outs]
    out_specs = [o[1] for o in outs]
    return pl.pallas_call(
        functools.partial(_in_kernel, kv_t=kv_t),
        grid=grid,
        in_specs=[row(d), pl.BlockSpec((nb_blk, 1, 3 * d), lambda i, j: (i, 0, 0)), full(w["g"]),
                  full(wa), full(wq), full(wk), full(wv), full(wc), full(wz), full(wf), full(w["ones_bd"]),
                  full(qg), full(kg), full(w["fb"])],
        out_specs=out_specs,
        out_shape=out_shape,
        compiler_params=_cparams(("arbitrary", "arbitrary")),
        name="in_proj_t" if kv_t else "in_proj",
    )(x, mod, w["g"], wa, wq, wk, wv, wc, wz, wf, w["ones_bd"], qg, kg, w["fb"])


def _mix_kernel(pa_ref, lng_ref, lnb_ref, ws_ref, bs_ref, ya_ref, *va_refs, tc, lowp):
    pa = pa_ref[0]
    n_c = pa.shape[0] // tc
    wa = pa.shape[1] // 2
    nh = wa // HEAD_DIM
    u = _gelu_tanh(pa[:, :wa])
    g = _gelu_tanh(pa[:, wa:])
    mu = jnp.mean(g, axis=-1, keepdims=True)
    var = jnp.mean(jnp.square(g - mu), axis=-1, keepdims=True)
    va = (g - mu) * lax.rsqrt(var + LN_EPS) * lng_ref[...] + lnb_ref[...]
    for va_ref in va_refs:
        va_ref[0] = va
    causal = _iota((tc, tc), 1) <= _iota((tc, tc), 0)
    lane_head = _iota((1, wa), 1) // HEAD_DIM
    mixes = [jnp.zeros((tc, wa), F32)] * n_c
    for h in range(nh):
        wc = jnp.where(causal, ws_ref[h, :tc, :tc], 0.0)
        vh = jnp.where(lane_head == h, va, 0.0)
        for ci in range(n_c):
            vc = vh[ci * tc:(ci + 1) * tc, :]
            if lowp:
                mixes[ci] = mixes[ci] + _mm(wc, vc)
            else:
                mixes[ci] = mixes[ci] + jnp.dot(wc, vc, preferred_element_type=F32)
    mix = jnp.concatenate([m + bs_ref[:tc, :] for m in mixes], axis=0)
    ya_ref[0] = u * mix


def _chunk_mix(pa, w, *, with_va):
    n_out = 2 if with_va else 1
    b, t, w2 = pa.shape
    chunk = w["a_ws"].shape[-1]
    tc = min(chunk, t)
    tm = tc * max(1, min(4, t // tc))
    assert t % tm == 0
    wa = w2 // 2
    full = lambda a: pl.BlockSpec(a.shape, lambda i, j: (0,) * a.ndim)
    return pl.pallas_call(
        functools.partial(_mix_kernel, tc=tc, lowp=tc >= 16),
        grid=(b, t // tm),
        in_specs=[pl.BlockSpec((1, tm, w2), lambda i, j: (i, j, 0)), full(w["a_ln_g"]), full(w["a_ln_b"]),
                  full(w["a_ws"]), full(w["a_bs_exp"])],
        out_specs=[pl.BlockSpec((1, tm, wa), lambda i, j: (i, j, 0))] * n_out,
        out_shape=[jax.ShapeDtypeStruct((b, t, wa), F32)] * n_out,
        compiler_params=_cparams(("arbitrary", "arbitrary")),
        name="chunk_mix",
    )(pa, w["a_ln_g"], w["a_ln_b"], w["a_ws"], w["a_bs_exp"])


def _cum_kernel(f_ref, ct_ref, ft_ref, ca_ref, carry, *, nh):
    @pl.when(pl.program_id(1) == 0)
    def _():
        carry[...] = jnp.zeros_like(carry)

    f = f_ref[0]
    tb = f.shape[0]
    tri = (_iota((tb, tb), 1) <= _iota((tb, tb), 0)).astype(F32)
    c = _mm_exact_l(tri, f) + carry[...]
    carry[...] = c[tb - 1:tb, :]
    c2 = c * LOG2E
    ct_ref[0] = c2.T[:nh, :]
    ft_ref[0] = f.T[:nh, :]
    src = _iota((LANES, LANES), 0)
    dst = _iota((LANES, LANES), 1)
    ca = jnp.zeros((tb, LANES), F32)
    for i, piece in enumerate(_split3(-c2)):
        place = ((dst == BIAS_LANES * src + i) & (src < nh)).astype(BF16)
        ca = ca + jnp.dot(piece, place, preferred_element_type=F32)
    ca_ref[0] = ca.astype(BF16)


def _cumsum_logf(f, nh, tb):
    b, t, _ = f.shape
    row = pl.BlockSpec((1, tb, LANES), lambda i, j: (i, j, 0))
    tr = pl.BlockSpec((1, nh, tb), lambda i, j: (i, 0, j))
    return pl.pallas_call(
        functools.partial(_cum_kernel, nh=nh),
        grid=(b, t // tb),
        in_specs=[row],
        out_specs=[tr, tr, row],
        out_shape=[jax.ShapeDtypeStruct((b, nh, t), F32), jax.ShapeDtypeStruct((b, nh, t), F32),
                   jax.ShapeDtypeStruct((b, t, LANES), BF16)],
        scratch_shapes=[pltpu.VMEM((1, LANES), F32)],
        compiler_params=_cparams(("arbitrary", "arbitrary")),
        name="cumsum_logf",
    )(f)


def _foxp_kernel(qi_ref, ki_ref, qt_ref, k_ref, vt_ref, ca_ref, ct_ref, o_ref, *scratch, nh, tq):
    m_scr, l_scr, acc_scr = scratch[:nh], scratch[nh:2 * nh], scratch[2 * nh:]
    qi = qi_ref[pl.program_id(1)]
    ki = ki_ref[pl.program_id(1)]

    @pl.when(ki == 0)
    def _():
        for h in range(nh):
            m_scr[h][...] = jnp.full_like(m_scr[h], NEG_BIG)
            l_scr[h][...] = jnp.zeros_like(l_scr[h])
            acc_scr[h][...] = jnp.zeros_like(acc_scr[h])

    sel_row = _iota((LANES, tq), 0)
    ones_rows = jnp.ones((16, tq), BF16)

    def block(masked):
        if masked:
            keep = _iota((tq, tq), 0) <= _iota((tq, tq), 1)
        def scores(h):
            pair = slice((h // 2) * LANES, (h // 2 + 1) * LANES)
            pick = ((sel_row >= BIAS_LANES * h) & (sel_row < BIAS_LANES * h + 3)).astype(BF16)
            k_cat = jnp.concatenate([k_ref[0, :, pair], ca_ref[0]], axis=1)
            q_cat = jnp.concatenate([qt_ref[0, h * LANES:(h + 1) * LANES, :], pick], axis=0)
            return jnp.dot(k_cat, q_cat, preferred_element_type=F32)

        s_next = scores(0)
        for h in range(nh):
            s = s_next
            if h + 1 < nh:
                s_next = scores(h + 1)
            if masked:
                s = jnp.where(keep, s, NEG_BIG)
            cq = ct_ref[0, h:h + 1, :]
            m_run = m_scr[h][...]
            m_new = jnp.maximum(m_run, jnp.max(s, axis=0, keepdims=True) + cq)
            alpha = jnp.exp2(m_run - m_new)
            p = jnp.exp2(s - (m_new - cq)).astype(BF16)
            v_ext = jnp.concatenate([vt_ref[0, h * HEAD_DIM:(h + 1) * HEAD_DIM, :], ones_rows], axis=0)
            pv = jnp.dot(v_ext, p, preferred_element_type=F32)
            l_scr[h][...] = alpha * l_scr[h][...] + pv[HEAD_DIM:HEAD_DIM + 1, :]
            m_scr[h][...] = m_new
            acc_scr[h][...] = acc_scr[h][...] * alpha + pv[:HEAD_DIM, :]

    @pl.when(ki < qi)
    def _():
        block(False)

    @pl.when(ki == qi)
    def _():
        block(True)
        for pr in range(nh // 2):
            sl = slice(pr * LANES, (pr + 1) * LANES)
            o_pair = jnp.concatenate([acc_scr[h][...] / l_scr[h][...] for h in (2 * pr, 2 * pr + 1)], axis=0)
            o_ref[0, :, sl] = o_pair.T


def _fox_prompt(qt, kb16, vt16, ca, ct, nh, tq):
    b, _, t = qt.shape
    wb = nh * HEAD_DIM
    nq = t // tq
    pairs = [(q, k) for q in range(nq) for k in range(q + 1)]
    qi_tab = jnp.asarray([p[0] for p in pairs], jnp.int32)
    ki_tab = jnp.asarray([p[1] for p in pairs], jnp.int32)
    q_t = lambda i, s, qi, ki: (i, 0, qi[s])
    kv_t = lambda i, s, qi, ki: (i, 0, ki[s])
    kv_r = lambda i, s, qi, ki: (i, ki[s], 0)
    return pl.pallas_call(
        functools.partial(_foxp_kernel, nh=nh, tq=tq),
        grid_spec=pltpu.PrefetchScalarGridSpec(
            num_scalar_prefetch=2,
            grid=(b, len(pairs)),
            in_specs=[pl.BlockSpec((1, nh * LANES, tq), q_t),
                      pl.BlockSpec((1, tq, wb), kv_r),
                      pl.BlockSpec((1, wb, tq), kv_t),
                      pl.BlockSpec((1, tq, LANES), kv_r),
                      pl.BlockSpec((1, nh, tq), q_t)],
            out_specs=pl.BlockSpec((1, tq, wb), lambda i, s, qi, ki: (i, qi[s], 0)),
            scratch_shapes=([pltpu.VMEM((1, tq), F32)] * (2 * nh) + [pltpu.VMEM((HEAD_DIM, tq), F32)] * nh),
        ),
        out_shape=jax.ShapeDtypeStruct((b, t, wb), F32),
        compiler_params=_cparams(("arbitrary", "arbitrary")),
        name="fox_prompt",
    )(qi_tab, ki_tab, qt, kb16, vt16, ca, ct)


def _cums_kernel(pt_ref, f_hbm, o_ref, buf, sems, *, layer, npg, nh):
    b = pl.program_id(0)
    slot = b % 2

    def page_copy(seq, j, sl):
        return pltpu.make_async_copy(f_hbm.at[layer, pt_ref[seq, j]], buf.at[sl, j], sems.at[sl])

    def start_all(seq, sl):
        def body(j, carry):
            page_copy(seq, j, sl).start()
            return carry
        lax.fori_loop(0, npg, body, 0, unroll=min(8, npg))

    def wait_all(seq, sl):
        def body(j, carry):
            page_copy(seq, j, sl).wait()
            return carry
        lax.fori_loop(0, npg, body, 0, unroll=min(8, npg))

    @pl.when(b == 0)
    def _():
        start_all(b, slot)

    @pl.when(b + 1 < pl.num_programs(0))
    def _():
        start_all(b + 1, 1 - slot)

    wait_all(b, slot)
    ps = buf.shape[-1]
    tri_keys = (_iota((ps, ps), 0) <= _iota((ps, ps), 1)).astype(F32)
    x = buf[slot].reshape(npg * nh, ps)
    y = _mm_exact_r(x, tri_keys).reshape(npg, nh, ps)
    run = jnp.zeros((nh, ps), F32)
    for p in range(npg):
        o_ref[0, p] = y[p] + run
        run = run + jnp.broadcast_to(y[p][:, ps - 1:ps], (nh, ps))


def _cumsum_pages(page_table, logf_t, layer):
    bs, npg = page_table.shape
    _, _, nh, ps = logf_t.shape
    return pl.pallas_call(
        functools.partial(_cums_kernel, layer=layer, npg=npg, nh=nh),
        grid_spec=pltpu.PrefetchScalarGridSpec(
            num_scalar_prefetch=1,
            grid=(bs,),
            in_specs=[pl.BlockSpec(memory_space=pl.ANY)],
            out_specs=pl.BlockSpec((1, npg, nh, ps), lambda i, pt: (i, 0, 0, 0)),
            scratch_shapes=[pltpu.VMEM((2, npg, nh, ps), F32), pltpu.SemaphoreType.DMA((2,))],
        ),
        out_shape=jax.ShapeDtypeStruct((bs, npg, nh, ps), F32),
        compiler_params=_cparams(("arbitrary",)),
        name="cumsum_pages",
    )(page_table, logf_t)


def _foxs_kernel(pt_ref, q_ref, kn_ref, vn_ref, fn_ref, cs_ref, cl_ref, *rest, nh, ts, pp):
    kt_refs = rest[:pp]
    vt_refs = rest[pp:2 * pp]
    o_ref = rest[2 * pp]
    m_scr, l_scr, acc_scr, cq_scr = rest[2 * pp + 1:]
    s_idx = pl.program_id(1)
    n_steps = pl.num_programs(1)
    nr = nh * ts
    wb = nh * HEAD_DIM
    ps = cs_ref.shape[-1]
    row_head = _iota((nr, 1), 0) // ts
    row_q = _iota((nr, 1), 0) % ts

    def rows_from_heads(x):
        return jnp.concatenate([jnp.broadcast_to(x[h:h + 1, :], (ts, x.shape[1])) for h in range(nh)], axis=0)

    q = q_ref[0]
    q_rep = jnp.concatenate([q] * nh, axis=0)
    qbd = jnp.where(_iota((nr, wb), 1) // HEAD_DIM == row_head, q_rep, 0.0).astype(BF16)

    fn = fn_ref[0]
    f_pad = jnp.concatenate([fn, jnp.zeros((LANES - ts, LANES), F32)], axis=0)
    a_new = rows_from_heads(f_pad.T[:nh, :])
    tri = (_iota((LANES, LANES), 0) <= _iota((LANES, LANES), 1)).astype(F32)
    c_new = _mm_exact_r(a_new, tri)

    @pl.when(s_idx == 0)
    def _():
        m_scr[...] = jnp.full_like(m_scr, NEG_BIG)
        l_scr[...] = jnp.zeros_like(l_scr)
        acc_scr[...] = jnp.zeros_like(acc_scr)
        total = rows_from_heads(cl_ref[0, 0][:, ps - 1:ps])
        own = jnp.sum(jnp.where(_iota((nr, LANES), 1) == row_q, c_new, 0.0), axis=-1, keepdims=True)
        cq_scr[...] = total + own

    cq = cq_scr[...]

    def online(s_list, v_list, v_is_t):
        m_prev = m_scr[...]
        m_new = m_prev
        for s in s_list:
            m_new = jnp.maximum(m_new, jnp.max(s, axis=-1, keepdims=True))
        alpha = jnp.exp(m_prev - m_new)
        l_add = jnp.zeros_like(m_prev)
        pv = jnp.zeros((nr, wb), F32)
        for s, v in zip(s_list, v_list):
            p = jnp.exp(s - m_new)
            l_add = l_add + jnp.sum(p, axis=-1, keepdims=True)
            if v_is_t:
                pv = pv + lax.dot_general(p.astype(BF16), v, (((1,), (1,)), ((), ())),
                                          preferred_element_type=F32)
            else:
                pv = pv + jnp.dot(p.astype(BF16), v, preferred_element_type=F32)
        m_scr[...] = m_new
        l_scr[...] = alpha * l_scr[...] + l_add
        acc_scr[...] = alpha * acc_scr[...] + pv

    s_list, v_list = [], []
    for j in range(pp):
        ck = rows_from_heads(cs_ref[0, j])
        s = jnp.dot(qbd, kt_refs[j][...].astype(BF16), preferred_element_type=F32) + cq - ck
        s_list.append(s)
        v_list.append(vt_refs[j][...].astype(BF16))
    online(s_list, v_list, True)

    @pl.when(s_idx == n_steps - 1)
    def _():
        kn = jnp.concatenate([kn_ref[0], jnp.zeros((LANES - ts, wb), F32)], axis=0).astype(BF16)
        vn = jnp.concatenate([vn_ref[0], jnp.zeros((LANES - ts, wb), F32)], axis=0).astype(BF16)
        total = rows_from_heads(cl_ref[0, 0][:, ps - 1:ps])
        s = lax.dot_general(qbd, kn, (((1,), (1,)), ((), ())), preferred_element_type=F32) + cq - (total + c_new)
        s = jnp.where(_iota((nr, LANES), 1) <= row_q, s, NEG_BIG)
        online([s], [vn], False)
        o_full = acc_scr[...] / l_scr[...]
        lane_head = _iota((1, wb), 1) // HEAD_DIM
        out = jnp.zeros((ts, wb), F32)
        for h in range(nh):
            out = out + jnp.where(lane_head == h, o_full[h * ts:(h + 1) * ts, :], 0.0)
        o_ref[0] = out


def _fox_sample(page_table, q, kn, vn, fn, cs, kt_pool, vt_pool, layer, nh):
    bs, ts, wb = q.shape
    npg = page_table.shape[1]
    ps = kt_pool.shape[-1]
    pp = min(PAGES_PER_STEP, npg)
    assert npg % pp == 0
    nr = nh * ts

    def page_spec(j):
        return pl.BlockSpec((None, None, wb, ps), lambda b, s, pt: (layer, pt[b, s * pp + j], 0, 0))

    row = lambda width: pl.BlockSpec((1, ts, width), lambda b, s, pt: (b, 0, 0))
    in_specs = [row(wb), row(wb), row(wb), row(LANES),
                pl.BlockSpec((1, pp, nh, ps), lambda b, s, pt: (b, s, 0, 0)),
                pl.BlockSpec((1, 1, nh, ps), lambda b, s, pt: (b, npg - 1, 0, 0))]
    in_specs += [page_spec(j) for j in range(pp)] * 2
    return pl.pallas_call(
        functools.partial(_foxs_kernel, nh=nh, ts=ts, pp=pp),
        grid_spec=pltpu.PrefetchScalarGridSpec(
            num_scalar_prefetch=1,
            grid=(bs, npg // pp),
            in_specs=in_specs,
            out_specs=row(wb),
            scratch_shapes=[pltpu.VMEM((nr, 1), F32), pltpu.VMEM((nr, 1), F32), pltpu.VMEM((nr, wb), F32),
                            pltpu.VMEM((nr, 1), F32)],
        ),
        out_shape=jax.ShapeDtypeStruct((bs, ts, wb), F32),
        compiler_params=_cparams(("arbitrary", "arbitrary")),
        name="fox_sample",
    )(page_table, q, kn, vn, fn, cs, cs, *([kt_pool] * pp), *([vt_pool] * pp))


def _seg_sum(x, ones_bd, lowp):
    if lowp:
        hi = x.astype(BF16)
        lo = (x - hi.astype(F32)).astype(BF16)
        return jnp.dot(hi, ones_bd, preferred_element_type=F32) + jnp.dot(lo, ones_bd, preferred_element_type=F32)
    return jnp.dot(x, ones_bd.astype(F32), precision=HIGHEST, preferred_element_type=F32)


def _rwkv_kernel(pc_ref, sh0_ref, s0_ref, mu_ref, w0_ref, wb_ref, a0_ref, ab_ref, kkw_ref, ka_ref, rk_ref,
                 gng_ref, gnb_ref, ones_ref, y_ref, sfin_ref, s_scr, prev_scr, *, nh, c, lowp):
    ti = pl.program_id(1)
    nseq, rows_seq, ws = pc_ref.shape
    n_chunks = rows_seq // c
    n = nseq * rows_seq
    wc = nh * HEAD_DIM
    lora = wb_ref.shape[0]

    @pl.when(ti == 0)
    def _():
        for sq in range(nseq):
            for p in range(nh // 2):
                s_scr[sq, p] = jnp.concatenate([s0_ref[sq, 2 * p], s0_ref[sq, 2 * p + 1]], axis=1)
        prev_scr[...] = sh0_ref[...]

    pc = pc_ref[...].reshape(n, ws)
    row = _iota((n, 1), 0)
    prev = pltpu.roll(pc, 1, axis=0)
    for sq in range(nseq):
        prev = jnp.where(row == sq * rows_seq, prev_scr[sq], prev)
        prev_scr[sq] = pc[(sq + 1) * rows_seq - 1:(sq + 1) * rows_seq, :]
    m = pc + (prev - pc) * mu_ref[...]
    r_all = m[:, :wc]
    k_raw = m[:, wc:2 * wc]
    v_all = m[:, 2 * wc:3 * wc]
    w_lo = m[:, 3 * wc:3 * wc + lora]
    a_lo = m[:, 3 * wc + lora:]
    hdot = lambda a, b: jnp.dot(a, b, precision=HIGHEST, preferred_element_type=F32)
    lw_all = -DECAY_MAX * _sigmoid(w0_ref[...] + hdot(jnp.tanh(w_lo), wb_ref[...]))
    a_all = _sigmoid(a0_ref[...] + hdot(a_lo, ab_ref[...]))
    kkr_all = k_raw * kkw_ref[...]
    k_all = k_raw * (1.0 + (a_all - 1.0) * ka_ref[...])
    seg = functools.partial(_seg_sum, ones_bd=ones_ref[...], lowp=lowp)
    kk_all = kkr_all * lax.rsqrt(jnp.maximum(seg(kkr_all * kkr_all), 1e-24))
    bonus_all = seg(r_all * k_all * rk_ref[...]) * v_all

    rr = _iota((n, n), 0)
    cc = _iota((n, n), 1)
    tri_chunks = ((cc <= rr) & (cc // c == rr // c)).astype(F32)
    if lowp:
        mm, mm_nt, mm_tn = _mm, _mm_nt, _mm_tn
        cum = _mm_exact_l(tri_chunks, lw_all)
    else:
        mm = lambda a, b: jnp.dot(a, b, preferred_element_type=F32)
        mm_nt = lambda a, b: lax.dot_general(a, b, (((1,), (1,)), ((), ())), preferred_element_type=F32)
        mm_tn = lambda a, b: lax.dot_general(a, b, (((0,), (0,)), ((), ())), preferred_element_type=F32)
        cum = hdot(tri_chunks, lw_all)
    e_pos = jnp.exp(cum)
    e_neg = jnp.exp(-cum)
    at_all = -kk_all * jnp.exp(cum - lw_all)
    bt_all = kk_all * a_all * e_neg
    kt_all = k_all * e_neg
    rt_all = r_all * e_pos

    def blockdiag(x, split):
        lane = _iota((1, x.shape[1]), 1)
        return jnp.concatenate([jnp.where(lane < split, x, 0.0), jnp.where(lane < split, 0.0, x)], axis=0)

    bd = lambda x: blockdiag(x, HEAD_DIM)
    bdc = lambda x: blockdiag(x, c)
    r2 = _iota((c, 2 * c), 0)
    c2 = _iota((c, 2 * c), 1) % c
    incl = c2 <= r2
    strict = c2 < r2
    eye = (c2 == r2).astype(F32)
    first = _iota((1, LANES), 1) < HEAD_DIM
    same_half = (_iota((LANES, LANES), 0) < HEAD_DIM) == (_iota((LANES, LANES), 1) < HEAD_DIM)

    keys = [(sq, g, p) for sq in range(nseq) for g in range(n_chunks) for p in range(nh // 2)]

    def pair_rows(x, k):
        sq, g, p = k
        return x[sq * rows_seq + g * c:sq * rows_seq + (g + 1) * c, p * LANES:(p + 1) * LANES]

    at = {k: pair_rows(at_all, k) for k in keys}
    bt = {k: pair_rows(bt_all, k) for k in keys}
    kt = {k: pair_rows(kt_all, k) for k in keys}
    rt = {k: pair_rows(rt_all, k) for k in keys}
    vv = {k: pair_rows(v_all, k) for k in keys}

    ar = {k: jnp.concatenate([at[k], rt[k]], axis=0) for k in keys}
    xb = {k: mm_nt(ar[k], bd(bt[k])) for k in keys}
    xk = {k: mm_nt(ar[k], bd(kt[k])) for k in keys}
    l_ab = {k: jnp.where(strict, xb[k][:c], 0.0) for k in keys}
    m_rb = {k: jnp.where(incl, xb[k][c:], 0.0) for k in keys}
    l_ak = {k: jnp.where(strict, xk[k][:c], 0.0) for k in keys}
    m_rk = {k: jnp.where(incl, xk[k][c:], 0.0) for k in keys}
    t_inv = {k: eye + l_ab[k] for k in keys}
    l_pow = l_ab
    span = 1
    while 2 * span < c:
        l_pow = {k: mm(l_pow[k], bdc(l_pow[k])) for k in keys}
        t_inv = {k: mm(t_inv[k], bdc(eye + l_pow[k])) for k in keys}
        span *= 2
    lv = {k: mm(l_ak[k], bd(vv[k])) for k in keys}
    w_m = {k: mm(t_inv[k], bd(at[k])) for k in keys}
    y_m = {k: mm(t_inv[k], bd(lv[k])) for k in keys}
    wtb = {k: jnp.where(same_half, mm_tn(w_m[k], bt[k]), 0.0) for k in keys}
    s_sq = {k: mm_tn(y_m[k], bt[k]) + mm_tn(vv[k], kt[k]) for k in keys}
    s_add = {k: jnp.where(first, s_sq[k][:HEAD_DIM], s_sq[k][HEAD_DIM:]) for k in keys}
    rw = {k: rt[k] + mm(m_rb[k], bd(w_m[k])) for k in keys}
    o_add = {k: mm(m_rb[k], bd(y_m[k])) + mm(m_rk[k], bd(vv[k])) for k in keys}

    o_pairs = {}
    for sq in range(nseq):
        for p in range(nh // 2):
            s = s_scr[sq, p]
            for g in range(n_chunks):
                k = (sq, g, p)
                o_pairs[k] = mm_nt(rw[k], bd(s)) + o_add[k]
                last = sq * rows_seq + (g + 1) * c - 1
                s = (s + mm(s, wtb[k]) + s_add[k]) * e_pos[last:last + 1, p * LANES:(p + 1) * LANES]
            s_scr[sq, p] = s

    o_all = jnp.concatenate(
        [jnp.concatenate([o_pairs[(sq, g, p)] for p in range(nh // 2)], axis=1)
         for sq in range(nseq) for g in range(n_chunks)], axis=0)
    mu_o = seg(o_all) * (1.0 / HEAD_DIM)
    dev = o_all - mu_o
    var = seg(dev * dev) * (1.0 / HEAD_DIM)
    y = dev * lax.rsqrt(var + GN_EPS) * gng_ref[...] + gnb_ref[...] + bonus_all
    y_ref[...] = y.reshape(nseq, rows_seq, wc)

    @pl.when(ti == pl.num_programs(1) - 1)
    def _():
        for sq in range(nseq):
            for p in range(nh // 2):
                s = s_scr[sq, p]
                sfin_ref[sq, 2 * p] = s[:, :HEAD_DIM]
                sfin_ref[sq, 2 * p + 1] = s[:, HEAD_DIM:]


def _rwkv(pc, shift0, wkv0, w, *, c, n_chunks, nseq):
    b, t, ws = pc.shape
    nh = wkv0.shape[1]
    wc = nh * HEAD_DIM
    rows = c * n_chunks
    assert t % rows == 0 and b % nseq == 0 and nh % 2 == 0
    full = lambda a: pl.BlockSpec(a.shape, lambda i, j: (0,) * a.ndim)
    names = ("c_mu", "c_w0", "c_wb", "c_a0", "c_ab", "c_kk", "c_ka", "c_rk", "c_gn_g", "c_gn_b", "ones_c")
    params = [w[n] for n in names]
    state_spec = pl.BlockSpec((nseq, nh, HEAD_DIM, HEAD_DIM), lambda i, j: (i, 0, 0, 0))
    return pl.pallas_call(
        functools.partial(_rwkv_kernel, nh=nh, c=c, lowp=c >= 16),
        grid=(b // nseq, t // rows),
        in_specs=[pl.BlockSpec((nseq, rows, ws), lambda i, j: (i, j, 0)),
                  pl.BlockSpec((nseq, 1, ws), lambda i, j: (i, 0, 0)),
                  state_spec] + [full(p) for p in params],
        out_specs=[pl.BlockSpec((nseq, rows, wc), lambda i, j: (i, j, 0)), state_spec],
        out_shape=[jax.ShapeDtypeStruct((b, t, wc), F32), jax.ShapeDtypeStruct((b, nh, HEAD_DIM, HEAD_DIM), F32)],
        scratch_shapes=[pltpu.VMEM((nseq, nh // 2, HEAD_DIM, LANES), F32), pltpu.VMEM((nseq, 1, ws), F32)],
        compiler_params=_cparams(("arbitrary", "arbitrary")),
        name="rwkv7",
    )(pc, shift0, wkv0, *params)


def _out_kernel(ya_ref, yb_ref, yc_ref, z_ref, x_ref, mod_ref, wo_ref, o_ref):
    x = x_ref[...]
    nb, tt, d = x.shape
    m = nb * tt
    z = z_ref[...]
    wa = ya_ref.shape[-1]
    wb = yb_ref.shape[-1]
    ya = (ya_ref[...] * z[:, :, :wa]).reshape(m, wa).astype(BF16)
    yb = (yb_ref[...] * z[:, :, wa:wa + wb]).reshape(m, wb).astype(BF16)
    yc = (yc_ref[...] * z[:, :, wa + wb:]).reshape(m, -1).astype(BF16)
    y = (jnp.dot(ya, wo_ref[:wa, :], preferred_element_type=F32)
         + jnp.dot(yb, wo_ref[wa:wa + wb, :], preferred_element_type=F32)
         + jnp.dot(yc, wo_ref[wa + wb:, :], preferred_element_type=F32))
    o_ref[...] = x + mod_ref[...][:, :, 2 * d:] * y.reshape(nb, tt, d)


def _out_proj(ya, yb, yc, z, x, mod, wo, *, nb_blk, t_blk):
    b, t, d = x.shape
    row = lambda a: pl.BlockSpec((nb_blk, t_blk, a.shape[-1]), lambda i, j: (i, j, 0))
    return pl.pallas_call(
        _out_kernel,
        grid=(b // nb_blk, t // t_blk),
        in_specs=[row(ya), row(yb), row(yc), row(z), row(x),
                  pl.BlockSpec((nb_blk, 1, 3 * d), lambda i, j: (i, 0, 0)),
                  pl.BlockSpec(wo.shape, lambda i, j: (0, 0))],
        out_specs=row(x),
        out_shape=jax.ShapeDtypeStruct((b, t, d), F32),
        compiler_params=_cparams(("arbitrary", "arbitrary")),
        name="out_proj",
    )(ya, yb, yc, z, x, mod, wo)


def _layer_weights(l, w_in, w_out, g_norm, a_ln_g, a_ln_b, a_ws, a_bs, b_qg, b_kg, b_fb, c_mu, c_w0, c_wb,
                   c_a0, c_ab, c_kk, c_ka, c_rk, c_gn_g, c_gn_b):
    wa_w = a_ln_g.shape[-1]
    nhb = b_fb.shape[-1]
    wb_w = nhb * HEAD_DIM
    wc_w = c_w0.shape[-1]
    lora_w, lora_a = c_wb.shape[1], c_ab.shape[1]
    o_q = 2 * wa_w
    o_k = o_q + wb_w
    o_v = o_k + wb_w
    o_f = o_v + wb_w
    o_c = o_f + nhb
    o_z = o_c + 3 * wc_w + lora_w + lora_a
    wi = w_in[l]
    row = lambda a: a.reshape(1, -1)
    head_lane = jnp.arange(wb_w) // HEAD_DIM
    head_lane_c = jnp.arange(wc_w) // HEAD_DIM
    w = dict(
        g=row(g_norm[l]),
        wa=wi[:, :o_q].astype(BF16), wq=wi[:, o_q:o_k].astype(BF16),
        wk=wi[:, o_k:o_v].astype(BF16), wv=wi[:, o_v:o_f].astype(BF16),
        wf=jnp.pad(wi[:, o_f:o_c], ((0, 0), (0, LANES - nhb))).astype(BF16),
        wc=wi[:, o_c:o_z].astype(BF16), wz=wi[:, o_z:].astype(BF16),
        wo=w_out[l].astype(BF16),
        ones_bd=(head_lane[:, None] == head_lane[None, :]).astype(BF16),
        ones_c=(head_lane_c[:, None] == head_lane_c[None, :]).astype(BF16),
        qg_row=row(jnp.tile(b_qg[l], nhb)), kg_row=row(jnp.tile(b_kg[l], nhb)),
        qg_col=b_qg[l].reshape(HEAD_DIM, 1), kg_col=b_kg[l].reshape(HEAD_DIM, 1),
        fb=row(jnp.pad(b_fb[l], (0, LANES - nhb))),
        a_ln_g=row(a_ln_g[l]), a_ln_b=row(a_ln_b[l]), a_ws=a_ws[l],
        a_bs_exp=jnp.repeat(a_bs[l].T, HEAD_DIM, axis=1),
        c_mu=row(c_mu[l]), c_w0=row(c_w0[l]), c_wb=c_wb[l], c_a0=row(c_a0[l]), c_ab=c_ab[l],
        c_kk=row(c_kk[l]), c_ka=row(c_ka[l]), c_rk=row(c_rk[l]), c_gn_g=row(c_gn_g[l]), c_gn_b=row(c_gn_b[l]),
    )
    w["wq_t"] = w["wq"].T
    w["wk_t"] = w["wk"].T
    w["wv_t"] = w["wv"].T
    return w, nhb


def kernel(x_prompt, x_sample, c_prompt, c_sample, cache_fox_k, cache_fox_v, cache_fox_logf, page_table, state_wkv, state_shift, w_ada, b_ada, g_norm, w_in, w_out, a_ln_g, a_ln_b, a_ws, a_bs, b_qg, b_kg, b_fb, c_mu, c_w0, c_wb, c_a0, c_ab, c_kk, c_ka, c_rk, c_gn_g, c_gn_b):
    nl = w_in.shape[0]
    bp, tp, d = x_prompt.shape
    bs, ts, _ = x_sample.shape
    nhc = state_wkv.shape[2]
    n_pool, ps = cache_fox_k.shape[1], cache_fox_k.shape[2]

    n_c = bp + bs
    n_pad = -(-n_c // 8) * 8
    c_all = jnp.pad(jnp.concatenate([c_prompt, c_sample], axis=0), ((0, n_pad - n_c), (0, 0)))
    mod_all = _ada_mod(c_all, w_ada, b_ada)

    kt_pool = jnp.transpose(cache_fox_k, (0, 1, 3, 4, 2)).reshape(nl, n_pool, -1, ps)
    vt_pool = jnp.transpose(cache_fox_v, (0, 1, 3, 4, 2)).reshape(nl, n_pool, -1, ps)
    logf_t = jnp.transpose(cache_fox_logf, (0, 1, 3, 2))

    tq = min(512, tp)
    t_blk = min(512, tp)
    rw_chunk = min(HEAD_DIM, tp)
    rw_chunks_per_step = max(1, min(8, tp // rw_chunk))
    xp, xs = x_prompt, x_sample
    outs = {k: [] for k in ("kp", "vp", "fp", "ks", "vs", "fs", "cvs", "wp", "wsl", "sp", "ssl")}
    for l in range(nl):
        w, nhb = _layer_weights(l, w_in, w_out, g_norm, a_ln_g, a_ln_b, a_ws, a_bs, b_qg, b_kg, b_fb, c_mu,
                                c_w0, c_wb, c_a0, c_ab, c_kk, c_ka, c_rk, c_gn_g, c_gn_b)
        mod_p = mod_all[l, :bp][:, None, :]
        mod_s = mod_all[l, bp:n_c][:, None, :]

        wp_t = dict(w, wq=w["wq_t"], wk=w["wk_t"], wv=w["wv_t"])
        pa, qt, kt, vt, kb16, vt16, f, pc, z = _in_proj(xp, mod_p, wp_t, kv_t=True, nb_blk=1, t_blk=t_blk)
        (ya,) = _chunk_mix(pa, w, with_va=False)
        ct, ft, ca = _cumsum_logf(f, nhb, tq)
        yb = _fox_prompt(qt, kb16, vt16, ca, ct, nhb, tq)
        yc, wkv_p = _rwkv(pc, jnp.zeros((bp, 1, pc.shape[-1]), F32), jnp.zeros((bp, nhc, HEAD_DIM, HEAD_DIM), F32),
                          w, c=rw_chunk, n_chunks=rw_chunks_per_step, nseq=1)
        xp = _out_proj(ya, yb, yc, z, xp, mod_p, w["wo"], nb_blk=1, t_blk=t_blk)
        outs["kp"].append(jnp.transpose(kt.reshape(bp, nhb, HEAD_DIM, tp), (0, 3, 1, 2)))
        outs["vp"].append(jnp.transpose(vt.reshape(bp, nhb, HEAD_DIM, tp), (0, 3, 1, 2)))
        outs["fp"].append(jnp.transpose(ft, (0, 2, 1)))
        outs["wp"].append(wkv_p)
        outs["sp"].append(pc[:, -1, :])

        pa, q, kn, vn, fn, pc, z = _in_proj(xs, mod_s, w, kv_t=False, nb_blk=bs, t_blk=ts)
        ya, va = _chunk_mix(pa, w, with_va=True)
        cs = _cumsum_pages(page_table, logf_t, l)
        yb = _fox_sample(page_table, q, kn, vn, fn, cs, kt_pool, vt_pool, l, nhb)
        yc, wkv_s = _rwkv(pc, state_shift[l][:, None, :], state_wkv[l], w, c=ts, n_chunks=1, nseq=min(8, bs))
        xs = _out_proj(ya, yb, yc, z, xs, mod_s, w["wo"], nb_blk=bs, t_blk=ts)
        outs["ks"].append(kn.reshape(bs, ts, nhb, HEAD_DIM))
        outs["vs"].append(vn.reshape(bs, ts, nhb, HEAD_DIM))
        outs["fs"].append(fn[:, :, :nhb])
        outs["cvs"].append(va.reshape(bs, ts, -1, HEAD_DIM))
        outs["wsl"].append(wkv_s)
        outs["ssl"].append(pc[:, -1, :])

    st = lambda k: jnp.stack(outs[k])
    return (xp, xs, st("kp"), st("vp"), st("fp"), st("ks"), st("vs"), st("fs"), st("cvs"),
            st("wp"), st("wsl"), st("sp"), st("ssl"))
```
